```python
import math
import jax, jax.numpy as jnp
from jax import lax
import numpy as np

D_MODEL = 4096
BATCH = 1
SEQ = 8192
DEPTH = 2
DEC_BATCH = 2
DEC_SEQ = 8192
PAST_LEN = 128

D_FF = 11008
A_HEADS = 8
A_DH = 128
A_W = A_HEADS * 2 * A_DH
B_GROUPS = ((128, 1), (512, 4), (2048, 16))
B_HEADS_PER_GROUP = 6
B_HEADS = B_HEADS_PER_GROUP * len(B_GROUPS)
B_DH = 128
B_W = B_HEADS * B_DH
B_OUT = B_HEADS_PER_GROUP * B_DH
B_BLOCK = 64
C_HEADS = 16
C_Q_RANK = 1024
C_KV_RANK = 512
C_NOPE = 128
C_ROPE_DIM = 64
C_VDIM = 128
C_OUT = C_HEADS * C_VDIM
N_BRANCH = 3
MLA_OFF = 3 * A_W + 3 * B_W
IN_SPLITS = (A_W, 2 * A_W, 3 * A_W,
             3 * A_W + B_W, 3 * A_W + 2 * B_W, MLA_OFF,
             MLA_OFF + C_Q_RANK, MLA_OFF + C_Q_RANK + C_KV_RANK,
             MLA_OFF + C_Q_RANK + C_KV_RANK + C_ROPE_DIM)
IN_COLS = MLA_OFF + C_Q_RANK + C_KV_RANK + C_ROPE_DIM + N_BRANCH * D_MODEL

ROPE_THETA = 10000.0
RMS_EPS = 1e-6
Q_BLOCK = 128

kernel_name = "hybrid_bidir_diff_dilated_mla_macaron"


def _rmsnorm(x, g):
    xf = x.astype(jnp.float32)
    y = xf * lax.rsqrt(jnp.mean(xf * xf, axis=-1, keepdims=True) + RMS_EPS)
    return (y * g.astype(jnp.float32)).astype(x.dtype)


def _rope(x, pos):
    d = x.shape[-1]
    half = d // 2
    inv = 1.0 / (ROPE_THETA ** (jnp.arange(half, dtype=jnp.float32) * (2.0 / d)))
    ang = pos.astype(jnp.float32)[:, None] * inv[None, :]
    shape = (x.shape[1],) + (1,) * (x.ndim - 3) + (half,)
    cos = jnp.cos(ang).reshape(shape)
    sin = jnp.sin(ang).reshape(shape)
    xf = x.astype(jnp.float32)
    x1, x2 = xf[..., :half], xf[..., half:]
    return jnp.concatenate([x1 * cos - x2 * sin, x2 * cos + x1 * sin], axis=-1).astype(x.dtype)


def _swiglu(h, wg, wu, wd):
    return (jax.nn.silu(h @ wg) * (h @ wu)) @ wd


def _dense_attention(q, k, v, scale):
    B, S, H, dq = q.shape
    nb = S // Q_BLOCK
    qb = q.reshape(B, nb, Q_BLOCK, H, dq).swapaxes(0, 1)

    def one(qblk):
        s = jnp.einsum('bqhd,bkhd->bhqk', qblk, k, preferred_element_type=jnp.float32) * scale
        p = jax.nn.softmax(s, axis=-1)
        return jnp.einsum('bhqk,bkhe->bqhe', p.astype(v.dtype), v)

    o = lax.map(one, qb)
    return o.swapaxes(0, 1).reshape(B, S, H, v.shape[-1])


def _diff_attention(q, k, v, lam):
    B, S, H, _, dh = q.shape
    nb = S // Q_BLOCK
    qb = q.reshape(B, nb, Q_BLOCK, H, 2, dh).swapaxes(0, 1)
    scale = dh ** -0.5

    def one(qblk):
        s = jnp.einsum('bqhjd,bkhjd->bhjqk', qblk, k, preferred_element_type=jnp.float32) * scale
        p = jax.nn.softmax(s, axis=-1)
        p = p[:, :, 0] - lam * p[:, :, 1]
        return jnp.einsum('bhqk,bkhe->bqhe', p.astype(v.dtype), v)

    o = lax.map(one, qb)
    return o.swapaxes(0, 1).reshape(B, S, H, v.shape[-1])


def _dilated_window_attention(q, k, v, dil, half):
    B, S, H, dh = q.shape
    L = S // dil
    N = B * dil

    def split(t):
        return t.reshape(B, L, dil, H, dh).transpose(0, 2, 1, 3, 4).reshape(N, L, H, dh)

    qs, ks, vs = split(q), split(k), split(v)
    blk = math.gcd(L, B_BLOCK)
    nb = L // blk
    span = blk + 2 * half
    padw = ((0, 0), (half, half), (0, 0), (0, 0))
    kp = jnp.pad(ks, padw)
    vp = jnp.pad(vs, padw)
    idx = jnp.arange(nb)[:, None] * blk + jnp.arange(span)[None, :]
    kw = kp[:, idx]
    vw = vp[:, idx]
    qb = qs.reshape(N, nb, blk, H, dh)
    s = jnp.einsum('nbqhd,nbkhd->nbhqk', qb, kw, preferred_element_type=jnp.float32) * (dh ** -0.5)
    rel = jnp.arange(span)[None, :] - half - jnp.arange(blk)[:, None]
    kpos = idx - half
    valid = (jnp.abs(rel) <= half)[None] & ((kpos >= 0) & (kpos < L))[:, None, :]
    s = jnp.where(valid[None, :, None], s, -jnp.inf)
    lse = jax.nn.logsumexp(s, axis=-1)
    p = jnp.exp(s - lse[..., None])
    o = jnp.einsum('nbhqk,nbkhd->nbqhd', p.astype(v.dtype), vw)
    o = o.reshape(B, dil, L, H, dh).transpose(0, 2, 1, 3, 4).reshape(B, S, H, dh)
    lse = lse.transpose(0, 1, 3, 2).reshape(B, dil, L, H).transpose(0, 2, 1, 3).reshape(B, S, H)
    return o, lse


def _mixer(u, pos, layer, w_in, a_qnorm, a_knorm, a_lq1, a_lk1, a_lq2, a_lk2, a_subln,
           b_qnorm, b_knorm, c_qa_norm, c_kva_norm, c_wq_up, c_wkv_up, c_qnorm, c_knorm,
           w_br_a, w_br_b, w_br_c, w_out):
    B, S, _ = u.shape
    proj = u @ w_in
    aq, ak, av, bq, bk, bv, cq, ckv, ckr, gates = jnp.split(proj, IN_SPLITS, axis=-1)

    lam_init = 0.8 - 0.6 * math.exp(-0.3 * layer)
    lam = (jnp.exp(jnp.sum(a_lq1.astype(jnp.float32) * a_lk1.astype(jnp.float32)))
           - jnp.exp(jnp.sum(a_lq2.astype(jnp.float32) * a_lk2.astype(jnp.float32))) + lam_init)
    qa = _rope(_rmsnorm(aq.reshape(B, S, A_HEADS, 2, A_DH), a_qnorm), pos)
    ka = _rope(_rmsnorm(ak.reshape(B, S, A_HEADS, 2, A_DH), a_knorm), pos)
    va = av.reshape(B, S, A_HEADS, 2 * A_DH)
    oa = _diff_attention(qa, ka, va, lam)
    ya = (_rmsnorm(oa, a_subln) * (1.0 - lam_init)).reshape(B, S, A_W) @ w_br_a

    qb = _rope(_rmsnorm(bq.reshape(B, S, B_HEADS, B_DH), b_qnorm), pos)
    kb = _rope(_rmsnorm(bk.reshape(B, S, B_HEADS, B_DH), b_knorm), pos)
    vb = bv.reshape(B, S, B_HEADS, B_DH)
    outs, lses = [], []
    for g, (win, dil) in enumerate(B_GROUPS):
        sl = slice(g * B_HEADS_PER_GROUP, (g + 1) * B_HEADS_PER_GROUP)
        o, lse = _dilated_window_attention(qb[:, :, sl], kb[:, :, sl], vb[:, :, sl], dil, win // (2 * dil))
        outs.append(o)
        lses.append(lse)
    wts = jax.nn.softmax(jnp.stack(lses), axis=0)
    ob = jnp.einsum('gbsh,gbshd->bshd', wts.astype(vb.dtype), jnp.stack(outs))
    yb = ob.reshape(B, S, B_OUT) @ w_br_b

    cqn = _rmsnorm(cq, c_qa_norm)
    qc = (cqn @ c_wq_up).reshape(B, S, C_HEADS, C_NOPE + C_ROPE_DIM)
    ckvn = _rmsnorm(ckv, c_kva_norm)
    kv = (ckvn @ c_wkv_up).reshape(B, S, C_HEADS, C_NOPE + C_VDIM)
    kc = jnp.concatenate([kv[..., :C_NOPE],
                          jnp.broadcast_to(ckr[:, :, None, :], (B, S, C_HEADS, C_ROPE_DIM))], axis=-1)
    vc = kv[..., C_NOPE:]
    qc = _rmsnorm(qc, c_qnorm)
    kc = _rmsnorm(kc, c_knorm)
    qc = jnp.concatenate([qc[..., :C_NOPE], _rope(qc[..., C_NOPE:], pos)], axis=-1)
    kc = jnp.concatenate([kc[..., :C_NOPE], _rope(kc[..., C_NOPE:], pos)], axis=-1)
    oc = _dense_attention(qc, kc, vc, (C_NOPE + C_ROPE_DIM) ** -0.5)
    yc = oc.reshape(B, S, C_OUT) @ w_br_c

    ga, gb, gc = jnp.split(jax.nn.sigmoid(gates), N_BRANCH, axis=-1)
    return (ga * ya + gb * yb + gc * yc) @ w_out


def _trunk(x, ffn1_norm, ffn1_wg, ffn1_wu, ffn1_wd, mix_norm, w_in,
           a_qnorm, a_knorm, a_lq1, a_lk1, a_lq2, a_lk2, a_subln,
           b_qnorm, b_knorm, c_qa_norm, c_kva_norm, c_wq_up, c_wkv_up, c_qnorm, c_knorm,
           w_br_a, w_br_b, w_br_c, w_out, ffn2_norm, ffn2_wg, ffn2_wu, ffn2_wd):
    pos = jnp.arange(x.shape[1], dtype=jnp.int32)
    for l in range(DEPTH):
        x = x + 0.5 * _swiglu(_rmsnorm(x, ffn1_norm[l]), ffn1_wg[l], ffn1_wu[l], ffn1_wd[l])
        x = x + _mixer(_rmsnorm(x, mix_norm[l]), pos, l, w_in[l],
                       a_qnorm[l], a_knorm[l], a_lq1[l], a_lk1[l], a_lq2[l], a_lk2[l], a_subln[l],
                       b_qnorm[l], b_knorm[l], c_qa_norm[l], c_kva_norm[l], c_wq_up[l], c_wkv_up[l],
                       c_qnorm[l], c_knorm[l], w_br_a[l], w_br_b[l], w_br_c[l], w_out[l])
        x = x + 0.5 * _swiglu(_rmsnorm(x, ffn2_norm[l]), ffn2_wg[l], ffn2_wu[l], ffn2_wd[l])
    return x


def setup_inputs(seed: int = 0) -> dict:
    key = jax.random.key(seed)
    keys = jax.random.split(key, 32)

    def nrm(i, shape, scale):
        return jax.random.normal(keys[i], shape, jnp.float32) * scale

    def gain(i, shape):
        return 1.0 + 0.05 * jax.random.normal(keys[i], shape, jnp.float32)

    L = DEPTH
    return {
        "x_prompt": nrm(0, (BATCH, SEQ, D_MODEL), 1.0),
        "x_sample": nrm(1, (DEC_BATCH, DEC_SEQ, D_MODEL), 1.0),
        "ffn1_norm": gain(2, (L, D_MODEL)),
        "ffn1_wg": nrm(3, (L, D_MODEL, D_FF), D_MODEL ** -0.5),
        "ffn1_wu": nrm(4, (L, D_MODEL, D_FF), D_MODEL ** -0.5),
        "ffn1_wd": nrm(5, (L, D_FF, D_MODEL), D_FF ** -0.5),
        "mix_norm": gain(6, (L, D_MODEL)),
        "w_in": nrm(7, (L, D_MODEL, IN_COLS), D_MODEL ** -0.5),
        "a_qnorm": gain(8, (L, A_DH)),
        "a_knorm": gain(9, (L, A_DH)),
        "a_lq1": nrm(10, (L, A_DH), 0.1),
        "a_lk1": nrm(11, (L, A_DH), 0.1),
        "a_lq2": nrm(12, (L, A_DH), 0.1),
        "a_lk2": nrm(13, (L, A_DH), 0.1),
        "a_subln": gain(14, (L, 2 * A_DH)),
        "b_qnorm": gain(15, (L, B_DH)),
        "b_knorm": gain(16, (L, B_DH)),
        "c_qa_norm": gain(17, (L, C_Q_RANK)),
        "c_kva_norm": gain(18, (L, C_KV_RANK)),
        "c_wq_up": nrm(19, (L, C_Q_RANK, C_HEADS * (C_NOPE + C_ROPE_DIM)), C_Q_RANK ** -0.5),
        "c_wkv_up": nrm(20, (L, C_KV_RANK, C_HEADS * (C_NOPE + C_VDIM)), C_KV_RANK ** -0.5),
        "c_qnorm": gain(21, (L, C_NOPE + C_ROPE_DIM)),
        "c_knorm": gain(22, (L, C_NOPE + C_ROPE_DIM)),
        "w_br_a": nrm(23, (L, A_W, D_MODEL), A_W ** -0.5),
        "w_br_b": nrm(24, (L, B_OUT, D_MODEL), B_OUT ** -0.5),
        "w_br_c": nrm(25, (L, C_OUT, D_MODEL), C_OUT ** -0.5),
        "w_out": nrm(26, (L, D_MODEL, D_MODEL), D_MODEL ** -0.5),
        "ffn2_norm": gain(27, (L, D_MODEL)),
        "ffn2_wg": nrm(28, (L, D_MODEL, D_FF), D_MODEL ** -0.5),
        "ffn2_wu": nrm(29, (L, D_MODEL, D_FF), D_MODEL ** -0.5),
        "ffn2_wd": nrm(30, (L, D_FF, D_MODEL), D_FF ** -0.5),
    }


def reference(x_prompt, x_sample, ffn1_norm, ffn1_wg, ffn1_wu, ffn1_wd, mix_norm, w_in,
              a_qnorm, a_knorm, a_lq1, a_lk1, a_lq2, a_lk2, a_subln,
              b_qnorm, b_knorm, c_qa_norm, c_kva_norm, c_wq_up, c_wkv_up, c_qnorm, c_knorm,
              w_br_a, w_br_b, w_br_c, w_out, ffn2_norm, ffn2_wg, ffn2_wu, ffn2_wd):
    weights = (ffn1_norm, ffn1_wg, ffn1_wu, ffn1_wd, mix_norm, w_in,
               a_qnorm, a_knorm, a_lq1, a_lk1, a_lq2, a_lk2, a_subln,
               b_qnorm, b_knorm, c_qa_norm, c_kva_norm, c_wq_up, c_wkv_up, c_qnorm, c_knorm,
               w_br_a, w_br_b, w_br_c, w_out, ffn2_norm, ffn2_wg, ffn2_wu, ffn2_wd)
    y_prompt = _trunk(x_prompt, *weights)
    y_sample = _trunk(x_sample, *weights)
    return (y_prompt, y_sample)
```

```python
import functools
import math

import jax
import jax.numpy as jnp
from jax import lax
from jax.experimental import pallas as pl
from jax.experimental.pallas import tpu as pltpu

F32 = jnp.float32
BF16 = jnp.bfloat16

RMS_EPS = 1e-6
ROPE_THETA = 10000.0
LANES = 128
V7X_VMEM_BYTES = 64 * 1024 * 1024
VMEM_LIMIT = V7X_VMEM_BYTES - 8 * 1024 * 1024

A_HEADS, A_DH = 8, 128
A_W = A_HEADS * 2 * A_DH
B_GROUPS = ((128, 1), (512, 4), (2048, 16))
B_HPG, B_DH = 6, 128
B_HEADS = B_HPG * len(B_GROUPS)
B_W = B_HEADS * B_DH
B_OUT = B_HPG * B_DH
C_HEADS, C_Q_RANK, C_KV_RANK, C_NOPE, C_ROPE, C_VDIM = 16, 1024, 512, 128, 64, 128
C_DQK = C_NOPE + C_ROPE
C_HPAD = 2 * LANES
C_OUT = C_HEADS * C_VDIM
AB_COLS = 3 * A_W + 3 * B_W
C_COLS = C_Q_RANK + C_KV_RANK + C_ROPE
C_COLS_PAD = 1792
NEG_BIG = -1e30


def _tile(n, t):
    t = min(n, t)
    assert n % t == 0, (n, t)
    return t


def _params(sem):
    return pltpu.CompilerParams(dimension_semantics=sem, vmem_limit_bytes=VMEM_LIMIT)


def _rmsnorm_kernel(x_ref, g_ref, o_ref):
    x = x_ref[...]
    y = x * lax.rsqrt(jnp.mean(x * x, axis=-1, keepdims=True) + RMS_EPS)
    o_ref[...] = (y * g_ref[...]).astype(o_ref.dtype)


def _rmsnorm(x, g, tm=512):
    T, D = x.shape
    tm = _tile(T, tm)
    return pl.pallas_call(
        _rmsnorm_kernel,
        grid=(T // tm,),
        in_specs=[pl.BlockSpec((tm, D), lambda i: (i, 0)), pl.BlockSpec((1, D), lambda i: (0, 0))],
        out_specs=pl.BlockSpec((tm, D), lambda i: (i, 0)),
        out_shape=jax.ShapeDtypeStruct((T, D), BF16),
        compiler_params=_params(("parallel",)),
        name="rmsnorm",
    )(x, g.reshape(1, D).astype(F32))


def _ffn_up_kernel(h_ref, wg_ref, wu_ref, o_ref):
    h = h_ref[...]
    g = jnp.dot(h, wg_ref[...], preferred_element_type=F32)
    u = jnp.dot(h, wu_ref[...], preferred_element_type=F32)
    o_ref[...] = (g * jax.nn.sigmoid(g) * u).astype(o_ref.dtype)


def _ffn_up(h, wg, wu, tm=1024, tn=512):
    T, D = h.shape
    N = wg.shape[1]
    tm, tn = _tile(T, tm), _tile(N, tn)
    return pl.pallas_call(
        _ffn_up_kernel,
        grid=(T // tm, N // tn),
        in_specs=[pl.BlockSpec((tm, D), lambda i, j: (i, 0)),
                  pl.BlockSpec((D, tn), lambda i, j: (0, j)),
                  pl.BlockSpec((D, tn), lambda i, j: (0, j))],
        out_specs=pl.BlockSpec((tm, tn), lambda i, j: (i, j)),
        out_shape=jax.ShapeDtypeStruct((T, N), BF16),
        compiler_params=_params(("parallel", "parallel")),
        name="ffn_up",
    )(h, wg, wu)


def _mm_res_kernel(a_ref, b_ref, x_ref, o_ref, *, scale, nk):
    k = pl.program_id(2)
    part = jnp.dot(a_ref[...], b_ref[...], preferred_element_type=F32)

    @pl.when(k == 0)
    def _():
        o_ref[...] = part

    @pl.when(k > 0)
    def _():
        o_ref[...] += part

    @pl.when(k == nk - 1)
    def _():
        o_ref[...] = x_ref[...] + scale * o_ref[...]


def _mm_res(a, b, x, scale, tm=1024, tn=1024, tk=1024):
    T, K = a.shape
    N = b.shape[1]
    tm, tn, tk = _tile(T, tm), _tile(N, tn), _tile(K, tk)
    nk = K // tk
    return pl.pallas_call(
        functools.partial(_mm_res_kernel, scale=scale, nk=nk),
        grid=(T // tm, N // tn, nk),
        in_specs=[pl.BlockSpec((tm, tk), lambda i, j, k: (i, k)),
                  pl.BlockSpec((tk, tn), lambda i, j, k: (k, j)),
                  pl.BlockSpec((tm, tn), lambda i, j, k: (i, j))],
        out_specs=pl.BlockSpec((tm, tn), lambda i, j, k: (i, j)),
        out_shape=jax.ShapeDtypeStruct((T, N), F32),
        compiler_params=_params(("parallel", "parallel", "arbitrary")),
        name="mm_residual",
    )(a, b, x)


def _rope128(y, cos, sin_signed):
    return y * cos + pltpu.roll(y, LANES // 2, 1) * sin_signed


def _proj_ab_kernel(u_ref, w_ref, g_ref, cos_ref, sin_ref, o_ref, *, rope_ranges, tn):
    j = pl.program_id(1)
    acc = jnp.dot(u_ref[...], w_ref[...], preferred_element_type=F32)
    is_rope = functools.reduce(jnp.logical_or, [(j >= lo) & (j < hi) for lo, hi in rope_ranges])

    @pl.when(is_rope)
    def _():
        for c in range(tn // LANES):
            sl = slice(c * LANES, (c + 1) * LANES)
            a = acc[:, sl]
            y = a * lax.rsqrt(jnp.mean(a * a, axis=-1, keepdims=True) + RMS_EPS) * g_ref[:, sl]
            o_ref[:, sl] = _rope128(y, cos_ref[...], sin_ref[...]).astype(o_ref.dtype)

    @pl.when(jnp.logical_not(is_rope))
    def _():
        o_ref[...] = acc.astype(o_ref.dtype)


def _proj_ab(u, w_ab, gains, cos, sin, seq, tm=1024, tn=256):
    T, D = u.shape
    N = w_ab.shape[1]
    tm = _tile(seq, tm)
    assert N % tn == 0 and A_W % tn == 0 and B_W % tn == 0
    npos = seq // tm
    a, b = A_W // tn, B_W // tn
    rope_ranges = ((0, 2 * a), (3 * a, 3 * a + 2 * b))
    return pl.pallas_call(
        functools.partial(_proj_ab_kernel, rope_ranges=rope_ranges, tn=tn),
        grid=(T // tm, N // tn),
        in_specs=[pl.BlockSpec((tm, D), lambda i, j: (i, 0)),
                  pl.BlockSpec((D, tn), lambda i, j: (0, j)),
                  pl.BlockSpec((1, tn), lambda i, j: (0, j)),
                  pl.BlockSpec((tm, LANES), lambda i, j: (i % npos, 0)),
                  pl.BlockSpec((tm, LANES), lambda i, j: (i % npos, 0))],
        out_specs=pl.BlockSpec((tm, tn), lambda i, j: (i, j)),
        out_shape=jax.ShapeDtypeStruct((T, N), BF16),
        compiler_params=_params(("parallel", "parallel")),
        name="proj_ab",
    )(u, w_ab, gains, cos, sin)


def _mm_kernel(a_ref, b_ref, o_ref):
    o_ref[...] = jnp.dot(a_ref[...], b_ref[...], preferred_element_type=F32).astype(o_ref.dtype)


def _mm(a, b, out_dtype, tm=1024, tn=256):
    T, K = a.shape
    N = b.shape[1]
    tm, tn = _tile(T, tm), _tile(N, tn)
    return pl.pallas_call(
        _mm_kernel,
        grid=(T // tm, N // tn),
        in_specs=[pl.BlockSpec((tm, K), lambda i, j: (i, 0)), pl.BlockSpec((K, tn), lambda i, j: (0, j))],
        out_specs=pl.BlockSpec((tm, tn), lambda i, j: (i, j)),
        out_shape=jax.ShapeDtypeStruct((T, N), out_dtype),
        compiler_params=_params(("parallel", "parallel")),
        name="mm",
    )(a, b)


def _rope64_in128(y, cos, sin_lo, sin_hi):
    q = C_ROPE // 2
    return y * cos + pltpu.roll(y, LANES - q, 1) * sin_lo + pltpu.roll(y, q, 1) * sin_hi


def _mla_q_kernel(cq_ref, ng_ref, w_ref, g_ref, cos_ref, slo_ref, shi_ref, o_ref, a_scr, *, tn):
    @pl.when(pl.program_id(1) == 0)
    def _():
        x = cq_ref[...]
        y = x * lax.rsqrt(jnp.mean(x * x, axis=-1, keepdims=True) + RMS_EPS) * ng_ref[...]
        a_scr[...] = y.astype(a_scr.dtype)

    acc = jnp.dot(a_scr[...], w_ref[...], preferred_element_type=F32)
    for h in range(tn // C_HPAD):
        a = acc[:, h * C_HPAD:(h + 1) * C_HPAD]
        r = lax.rsqrt(jnp.sum(a * a, axis=-1, keepdims=True) * (1.0 / C_DQK) + RMS_EPS)
        y = a * r * g_ref[...]
        o_ref[:, h * C_HPAD:h * C_HPAD + LANES] = y[:, :LANES].astype(o_ref.dtype)
        o_ref[:, h * C_HPAD + LANES:(h + 1) * C_HPAD] = _rope64_in128(
            y[:, LANES:], cos_ref[...], slo_ref[...], shi_ref[...]).astype(o_ref.dtype)


def _mla_q(pc, qa_gain, wq_pad, q_gain_pad, cos, slo, shi, seq, tm=512, tn=512):
    T = pc.shape[0]
    N = wq_pad.shape[1]
    tm, tn = _tile(seq, tm), _tile(N, tn)
    npos = seq // tm
    pos = lambda i, j: (i % npos, 0)
    return pl.pallas_call(
        functools.partial(_mla_q_kernel, tn=tn),
        grid=(T // tm, N // tn),
        in_specs=[pl.BlockSpec((tm, C_Q_RANK), lambda i, j: (i, 0)),
                  pl.BlockSpec((1, C_Q_RANK), lambda i, j: (0, 0)),
                  pl.BlockSpec((C_Q_RANK, tn), lambda i, j: (0, j)),
                  pl.BlockSpec((1, C_HPAD), lambda i, j: (0, 0)),
                  pl.BlockSpec((tm, LANES), pos), pl.BlockSpec((tm, LANES), pos), pl.BlockSpec((tm, LANES), pos)],
        out_specs=pl.BlockSpec((tm, tn), lambda i, j: (i, j)),
        out_shape=jax.ShapeDtypeStruct((T, N), BF16),
        scratch_shapes=[pltpu.VMEM((tm, C_Q_RANK), BF16)],
        compiler_params=_params(("parallel", "arbitrary")),
        name="mla_q_up",
    )(pc, qa_gain, wq_pad, q_gain_pad, cos, slo, shi)


def _mla_kv_kernel(ckv_ref, ng_ref, kr_ref, w_ref, gn_ref, gr_ref, cos_ref, slo_ref, shi_ref,
                   k_ref, v_ref, a_scr, *, heads):
    @pl.when(pl.program_id(1) == 0)
    def _():
        x = ckv_ref[...]
        y = x * lax.rsqrt(jnp.mean(x * x, axis=-1, keepdims=True) + RMS_EPS) * ng_ref[...]
        a_scr[...] = y.astype(a_scr.dtype)

    acc = jnp.dot(a_scr[...], w_ref[...], preferred_element_type=F32)
    kr = kr_ref[...]
    kr_ss = jnp.sum(kr * kr, axis=-1, keepdims=True)
    for h in range(heads):
        base = h * (C_NOPE + C_VDIM)
        kn = acc[:, base:base + C_NOPE]
        r = lax.rsqrt((jnp.sum(kn * kn, axis=-1, keepdims=True) + kr_ss) * (1.0 / C_DQK) + RMS_EPS)
        k_ref[:, h * C_HPAD:h * C_HPAD + LANES] = (kn * r * gn_ref[...]).astype(k_ref.dtype)
        k_ref[:, h * C_HPAD + LANES:(h + 1) * C_HPAD] = _rope64_in128(
            kr * r * gr_ref[...], cos_ref[...], slo_ref[...], shi_ref[...]).astype(k_ref.dtype)
        v_ref[:, h * C_VDIM:(h + 1) * C_VDIM] = acc[:, base + C_NOPE:base + C_NOPE + C_VDIM].astype(v_ref.dtype)


def _mla_kv(pc, kva_gain, wkv, k_gain_nope, k_gain_rope, cos, slo, shi, seq, tm=512, heads=2):
    T = pc.shape[0]
    tm = _tile(seq, tm)
    npos = seq // tm
    pos = lambda i, j: (i % npos, 0)
    hw = C_NOPE + C_VDIM
    return pl.pallas_call(
        functools.partial(_mla_kv_kernel, heads=heads),
        grid=(T // tm, C_HEADS // heads),
        in_specs=[pl.BlockSpec((tm, C_KV_RANK), lambda i, j: (i, C_Q_RANK // C_KV_RANK)),
                  pl.BlockSpec((1, C_KV_RANK), lambda i, j: (0, 0)),
                  pl.BlockSpec((tm, LANES), lambda i, j: (i, (C_Q_RANK + C_KV_RANK) // LANES)),
                  pl.BlockSpec((C_KV_RANK, heads * hw), lambda i, j: (0, j)),
                  pl.BlockSpec((1, LANES), lambda i, j: (0, 0)),
                  pl.BlockSpec((1, LANES), lambda i, j: (0, 0)),
                  pl.BlockSpec((tm, LANES), pos), pl.BlockSpec((tm, LANES), pos), pl.BlockSpec((tm, LANES), pos)],
        out_specs=[pl.BlockSpec((tm, heads * C_HPAD), lambda i, j: (i, j)),
                   pl.BlockSpec((tm, heads * C_VDIM), lambda i, j: (i, j))],
        out_shape=[jax.ShapeDtypeStruct((T, C_HEADS * C_HPAD), BF16),
                   jax.ShapeDtypeStruct((T, C_HEADS * C_VDIM), BF16)],
        scratch_shapes=[pltpu.VMEM((tm, C_KV_RANK), BF16)],
        compiler_params=_params(("parallel", "arbitrary")),
        name="mla_kv_up",
    )(pc, kva_gain, pc, wkv, k_gain_nope, k_gain_rope, cos, slo, shi)


def _nt_dot(a, b):
    return lax.dot_general(a, b, (((1,), (1,)), ((), ())), preferred_element_type=F32)


def _online_softmax_step(s, v, m_prev, l_prev, acc_ref, idx):
    m_new = jnp.maximum(m_prev, jnp.max(s, axis=-1, keepdims=True))
    alpha = jnp.exp(m_prev - m_new)
    p = jnp.exp(s - m_new)
    l_new = alpha * l_prev + jnp.sum(p, axis=-1, keepdims=True)
    acc_ref[idx] = acc_ref[idx] * alpha + jnp.dot(p.astype(v.dtype), v, preferred_element_type=F32)
    return m_new, l_new


def _diff_attn_kernel(q_ref, k_ref, v_ref, lq1_ref, lk1_ref, lq2_ref, lk2_ref, g_ref, o_ref, acc_ref,
                      *, tk, lam_init):
    tq = q_ref.shape[0]
    nk = k_ref.shape[0] // tk
    scale = A_DH ** -0.5
    q = q_ref[...]
    acc_ref[...] = jnp.zeros_like(acc_ref)

    def body(j, carry):
        m0, l0, m1, l1 = carry
        off = pl.multiple_of(j * tk, tk)
        k = k_ref[pl.ds(off, tk), :]
        v = v_ref[pl.ds(off, tk), :]
        m0, l0 = _online_softmax_step(_nt_dot(q[:, :A_DH], k[:, :A_DH]) * scale, v, m0, l0, acc_ref, 0)
        m1, l1 = _online_softmax_step(_nt_dot(q[:, A_DH:], k[:, A_DH:]) * scale, v, m1, l1, acc_ref, 1)
        return m0, l0, m1, l1

    neg = jnp.full((tq, 1), -jnp.inf, F32)
    zero = jnp.zeros((tq, 1), F32)
    _, l0, _, l1 = lax.fori_loop(0, nk, body, (neg, zero, neg, zero))

    lam = (jnp.exp(jnp.sum(lq1_ref[...] * lk1_ref[...], axis=-1, keepdims=True))
           - jnp.exp(jnp.sum(lq2_ref[...] * lk2_ref[...], axis=-1, keepdims=True)) + lam_init)
    o = acc_ref[0] / l0 - lam * (acc_ref[1] / l1)
    y = o * lax.rsqrt(jnp.mean(o * o, axis=-1, keepdims=True) + RMS_EPS) * g_ref[...]
    o_ref[...] = (y * (1.0 - lam_init)).astype(o_ref.dtype)


def _diff_attn(pab, lq1, lk1, lq2, lk2, subln, nb, seq, lam_init, tq=512, tk=512):
    T = pab.shape[0]
    tq, tk = _tile(seq, tq), _tile(seq, tk)
    nq = seq // tq
    hw = 2 * A_DH
    vec = lambda a: a.reshape(1, -1).astype(F32)
    small = pl.BlockSpec((1, A_DH), lambda b, h, i: (0, 0))
    return pl.pallas_call(
        functools.partial(_diff_attn_kernel, tk=tk, lam_init=lam_init),
        grid=(nb, A_HEADS, nq),
        in_specs=[pl.BlockSpec((tq, hw), lambda b, h, i: (b * nq + i, h)),
                  pl.BlockSpec((seq, hw), lambda b, h, i: (b, A_HEADS + h)),
                  pl.BlockSpec((seq, hw), lambda b, h, i: (b, 2 * A_HEADS + h)),
                  small, small, small, small,
                  pl.BlockSpec((1, hw), lambda b, h, i: (0, 0))],
        out_specs=pl.BlockSpec((tq, hw), lambda b, h, i: (b * nq + i, h)),
        out_shape=jax.ShapeDtypeStruct((T, A_W), BF16),
        scratch_shapes=[pltpu.VMEM((2, tq, hw), F32)],
        compiler_params=_params(("parallel", "parallel", "parallel")),
        name="diff_attn",
    )(pab, pab, pab, vec(lq1), vec(lk1), vec(lq2), vec(lk2), vec(subln))


def _mla_attn_kernel(q_ref, k_ref, v_ref, o_ref, acc_ref, *, tk):
    tq = q_ref.shape[0]
    nk = k_ref.shape[0] // tk
    scale = C_DQK ** -0.5
    q = q_ref[...]
    acc_ref[...] = jnp.zeros_like(acc_ref)

    def body(j, carry):
        off = pl.multiple_of(j * tk, tk)
        s = _nt_dot(q, k_ref[pl.ds(off, tk), :]) * scale
        return _online_softmax_step(s, v_ref[pl.ds(off, tk), :], carry[0], carry[1], acc_ref, 0)

    _, l = lax.fori_loop(0, nk, body, (jnp.full((tq, 1), -jnp.inf, F32), jnp.zeros((tq, 1), F32)))
    o_ref[...] = (acc_ref[0] / l).astype(o_ref.dtype)


def _mla_attn(qc, kc, vc, nb, seq, tq=512, tk=512):
    T = qc.shape[0]
    tq, tk = _tile(seq, tq), _tile(seq, tk)
    nq = seq // tq
    return pl.pallas_call(
        functools.partial(_mla_attn_kernel, tk=tk),
        grid=(nb, C_HEADS, nq),
        in_specs=[pl.BlockSpec((tq, C_HPAD), lambda b, h, i: (b * nq + i, h)),
                  pl.BlockSpec((seq, C_HPAD), lambda b, h, i: (b, h)),
                  pl.BlockSpec((seq, C_VDIM), lambda b, h, i: (b, h))],
        out_specs=pl.BlockSpec((tq, C_VDIM), lambda b, h, i: (b * nq + i, h)),
        out_shape=jax.ShapeDtypeStruct((T, C_OUT), BF16),
        scratch_shapes=[pltpu.VMEM((1, tq, C_VDIM), F32)],
        compiler_params=_params(("parallel", "parallel", "parallel")),
        name="mla_attn",
    )(qc, kc, vc)


def _window_attn_kernel(q_ref, kp_ref, km_ref, kn_ref, vp_ref, vm_ref, vn_ref, o_ref, lse_ref, *, half, sub_len):
    tq = q_ref.shape[0]
    i = pl.program_id(2)
    span = tq + 2 * half
    qpos = i * tq + lax.broadcasted_iota(jnp.int32, (tq, span), 0)
    kpos = i * tq - half + lax.broadcasted_iota(jnp.int32, (tq, span), 1)
    valid = (jnp.abs(kpos - qpos) <= half) & (kpos >= 0) & (kpos < sub_len)
    scale = B_DH ** -0.5
    for h in range(B_HPG):
        sl = slice(h * B_DH, (h + 1) * B_DH)
        k = jnp.concatenate([kp_ref[:, sl], km_ref[:, sl], kn_ref[:, sl]], axis=0)
        v = jnp.concatenate([vp_ref[:, sl], vm_ref[:, sl], vn_ref[:, sl]], axis=0)
        s = jnp.where(valid, _nt_dot(q_ref[:, sl], k) * scale, NEG_BIG)
        m = jnp.max(s, axis=-1, keepdims=True)
        p = jnp.exp(s - m)
        l = jnp.sum(p, axis=-1, keepdims=True)
        o = jnp.dot(p.astype(v.dtype), v, preferred_element_type=F32) / l
        o_ref[:, sl] = o.astype(o_ref.dtype)
        lse_ref[:, sl] = jnp.broadcast_to(m + jnp.log(l), (tq, B_DH))


def _window_attn(pab, g, nb, seq, tq=256):
    win, dil = B_GROUPS[g]
    half = win // (2 * dil)
    L = seq // dil
    tq = _tile(L, tq)
    assert tq % half == 0 and L % tq == 0 and seq % dil == 0
    T = pab.shape[0]
    nq = L // tq
    r = tq // half
    nh = L // half
    gw = B_HPG * B_DH
    cb = AB_COLS // gw
    assert AB_COLS % gw == 0 and (3 * A_W) % gw == 0
    qb0 = 3 * A_W // gw
    kb0, vb0 = qb0 + len(B_GROUPS), qb0 + 2 * len(B_GROUPS)
    view = pab.reshape(T // dil, dil * AB_COLS)

    def main(c0):
        return pl.BlockSpec((tq, gw), lambda b, c, i: (b * nq + i, c * cb + c0 + g))

    def prev(c0):
        return pl.BlockSpec((half, gw), lambda b, c, i: (b * nh + jnp.maximum(i * r - 1, 0), c * cb + c0 + g))

    def nxt(c0):
        return pl.BlockSpec((half, gw), lambda b, c, i: (b * nh + jnp.minimum((i + 1) * r, nh - 1), c * cb + c0 + g))

    out_spec = pl.BlockSpec((tq, gw), lambda b, c, i: (b * nq + i, c))
    o, lse = pl.pallas_call(
        functools.partial(_window_attn_kernel, half=half, sub_len=L),
        grid=(nb, dil, nq),
        in_specs=[main(qb0), prev(kb0), main(kb0), nxt(kb0), prev(vb0), main(vb0), nxt(vb0)],
        out_specs=[out_spec, out_spec],
        out_shape=[jax.ShapeDtypeStruct((T // dil, dil * gw), BF16),
                   jax.ShapeDtypeStruct((T // dil, dil * gw), F32)],
        compiler_params=_params(("parallel", "parallel", "parallel")),
        name=f"window_attn_d{dil}",
    )(view, view, view, view, view, view, view)
    return o.reshape(T, gw), lse.reshape(T, gw)


def _combine_kernel(o0, o1, o2, l0, l1, l2, o_ref):
    m = jnp.maximum(jnp.maximum(l0[...], l1[...]), l2[...])
    e0, e1, e2 = jnp.exp(l0[...] - m), jnp.exp(l1[...] - m), jnp.exp(l2[...] - m)
    num = e0 * o0[...].astype(F32) + e1 * o1[...].astype(F32) + e2 * o2[...].astype(F32)
    o_ref[...] = (num / (e0 + e1 + e2)).astype(o_ref.dtype)


def _combine(outs, lses, tm=512):
    T, W = outs[0].shape
    tm = _tile(T, tm)
    spec = pl.BlockSpec((tm, W), lambda i: (i, 0))
    return pl.pallas_call(
        _combine_kernel,
        grid=(T // tm,),
        in_specs=[spec] * 6,
        out_specs=spec,
        out_shape=jax.ShapeDtypeStruct((T, W), BF16),
        compiler_params=_params(("parallel",)),
        name="window_combine",
    )(*outs, *lses)


def _merge_kernel(u_ref, oa_ref, ob_ref, oc_ref, wga_ref, wgb_ref, wgc_ref, wa_ref, wb_ref, wc_ref, o_ref):
    u = u_ref[...]

    def branch(wg_ref, x_ref, w_ref):
        gate = jax.nn.sigmoid(jnp.dot(u, wg_ref[...], preferred_element_type=F32))
        return gate * jnp.dot(x_ref[...], w_ref[...], preferred_element_type=F32)

    o_ref[...] = (branch(wga_ref, oa_ref, wa_ref) + branch(wgb_ref, ob_ref, wb_ref)
                  + branch(wgc_ref, oc_ref, wc_ref)).astype(o_ref.dtype)


def _merge(u, oa, ob, oc, w_gates, wa, wb, wc, tm=512, tn=256):
    T, D = u.shape
    tm, tn = _tile(T, tm), _tile(D, tn)
    nj = D // tn
    row = lambda w: pl.BlockSpec((tm, w), lambda i, j: (i, 0))
    col = lambda k: pl.BlockSpec((k, tn), lambda i, j: (0, j))
    gate = lambda n: pl.BlockSpec((D, tn), lambda i, j: (0, n * nj + j))
    return pl.pallas_call(
        _merge_kernel,
        grid=(T // tm, nj),
        in_specs=[row(D), row(oa.shape[1]), row(ob.shape[1]), row(oc.shape[1]),
                  gate(0), gate(1), gate(2), col(wa.shape[0]), col(wb.shape[0]), col(wc.shape[0])],
        out_specs=pl.BlockSpec((tm, tn), lambda i, j: (i, j)),
        out_shape=jax.ShapeDtypeStruct((T, D), BF16),
        compiler_params=_params(("parallel", "parallel")),
        name="gated_merge",
    )(u, oa, ob, oc, w_gates, w_gates, w_gates, wa, wb, wc)


def _rope_tables(seq):
    pos = jnp.arange(seq, dtype=F32)[:, None]

    def ang(d):
        half = d // 2
        inv = 1.0 / (ROPE_THETA ** (jnp.arange(half, dtype=F32) * (2.0 / d)))
        return pos * inv[None, :]

    a = ang(A_DH)
    cos128 = jnp.concatenate([jnp.cos(a), jnp.cos(a)], axis=-1)
    sin128 = jnp.concatenate([-jnp.sin(a), jnp.sin(a)], axis=-1)
    c = ang(C_ROPE)
    z = jnp.zeros_like(c)
    cos64 = jnp.concatenate([jnp.cos(c), jnp.cos(c), z, z], axis=-1)
    sin_lo = jnp.concatenate([-jnp.sin(c), z, z, z], axis=-1)
    sin_hi = jnp.concatenate([z, jnp.sin(c), z, z], axis=-1)
    return cos128, sin128, cos64, sin_lo, sin_hi


def _pad_cols(w, n):
    return jnp.pad(w, ((0, 0), (0, n - w.shape[1])))


def _layer(x, l, nb, seq, tabs, p):
    cos128, sin128, cos64, sin_lo, sin_hi = tabs
    D = x.shape[1]
    ff = p["ffn1_wg"].shape[2]
    ffp = -(-ff // 1024) * 1024

    def ffn(x, norm, wg, wu, wd):
        h = _rmsnorm(x, norm[l])
        act = _ffn_up(h, _pad_cols(wg[l], ffp).astype(BF16), _pad_cols(wu[l], ffp).astype(BF16))
        wd_p = jnp.pad(wd[l], ((0, ffp - ff), (0, 0))).astype(BF16)
        return _mm_res(act, wd_p, x, 0.5)

    x = ffn(x, p["ffn1_norm"], p["ffn1_wg"], p["ffn1_wu"], p["ffn1_wd"])

    u = _rmsnorm(x, p["mix_norm"][l])
    w_in = p["w_in"][l]
    ones = jnp.ones
    gains = jnp.concatenate([
        jnp.tile(p["a_qnorm"][l], A_W // A_DH), jnp.tile(p["a_knorm"][l], A_W // A_DH), ones((A_W,), F32),
        jnp.tile(p["b_qnorm"][l], B_HEADS), jnp.tile(p["b_knorm"][l], B_HEADS), ones((B_W,), F32)]).reshape(1, AB_COLS)
    pab = _proj_ab(u, w_in[:, :AB_COLS].astype(BF16), gains, cos128, sin128, seq)
    pc = _mm(u, _pad_cols(w_in[:, AB_COLS:AB_COLS + C_COLS], C_COLS_PAD).astype(BF16), F32)

    lam_init = 0.8 - 0.6 * math.exp(-0.3 * l)
    oa = _diff_attn(pab, p["a_lq1"][l], p["a_lk1"][l], p["a_lq2"][l], p["a_lk2"][l], p["a_subln"][l],
                    nb, seq, lam_init)

    wres = [_window_attn(pab, g, nb, seq) for g in range(len(B_GROUPS))]
    ob = _combine([r[0] for r in wres], [r[1] for r in wres])

    row = lambda a: a.reshape(1, -1).astype(F32)
    wq = p["c_wq_up"][l].reshape(C_Q_RANK, C_HEADS, C_DQK)
    wq_pad = jnp.pad(wq, ((0, 0), (0, 0), (0, C_HPAD - C_DQK))).reshape(C_Q_RANK, C_HEADS * C_HPAD).astype(BF16)
    qg = p["c_qnorm"][l]
    kg = p["c_knorm"][l]
    pad_to = lambda v, n: jnp.pad(v, (0, n - v.shape[0]))
    qc = _mla_q(pc, row(p["c_qa_norm"][l]), wq_pad, row(pad_to(qg, C_HPAD)), cos64, sin_lo, sin_hi, seq)
    kc, vc = _mla_kv(pc, row(p["c_kva_norm"][l]), p["c_wkv_up"][l].astype(BF16), row(kg[:C_NOPE]),
                     row(pad_to(kg[C_NOPE:], LANES)), cos64, sin_lo, sin_hi, seq)
    oc = _mla_attn(qc, kc, vc, nb, seq)

    m = _merge(u, oa, ob, oc, w_in[:, AB_COLS + C_COLS:].astype(BF16), p["w_br_a"][l].astype(BF16),
               p["w_br_b"][l].astype(BF16), p["w_br_c"][l].astype(BF16))
    x = _mm_res(m, p["w_out"][l].astype(BF16), x, 1.0, tm=1024, tn=512, tk=D)

    return ffn(x, p["ffn2_norm"], p["ffn2_wg"], p["ffn2_wu"], p["ffn2_wd"])


def kernel(x_prompt, x_sample, ffn1_norm, ffn1_wg, ffn1_wu, ffn1_wd, mix_norm, w_in, a_qnorm, a_knorm, a_lq1, a_lk1, a_lq2, a_lk2, a_subln, b_qnorm, b_knorm, c_qa_norm, c_kva_norm, c_wq_up, c_wkv_up, c_qnorm, c_knorm, w_br_a, w_br_b, w_br_c, w_out, ffn2_norm, ffn2_wg, ffn2_wu, ffn2_wd):
    p = dict(ffn1_norm=ffn1_norm, ffn1_wg=ffn1_wg, ffn1_wu=ffn1_wu, ffn1_wd=ffn1_wd, mix_norm=mix_norm, w_in=w_in,
             a_qnorm=a_qnorm, a_knorm=a_knorm, a_lq1=a_lq1, a_lk1=a_lk1, a_lq2=a_lq2, a_lk2=a_lk2, a_subln=a_subln,
             b_qnorm=b_qnorm, b_knorm=b_knorm, c_qa_norm=c_qa_norm, c_kva_norm=c_kva_norm, c_wq_up=c_wq_up,
             c_wkv_up=c_wkv_up, c_qnorm=c_qnorm, c_knorm=c_knorm, w_br_a=w_br_a, w_br_b=w_br_b, w_br_c=w_br_c,
             w_out=w_out, ffn2_norm=ffn2_norm, ffn2_wg=ffn2_wg, ffn2_wu=ffn2_wu, ffn2_wd=ffn2_wd)
    bp, seq, D = x_prompt.shape
    bs, seq_s, _ = x_sample.shape
    assert seq == seq_s
    nb = bp + bs
    x = jnp.concatenate([x_prompt.reshape(bp * seq, D), x_sample.reshape(bs * seq, D)], axis=0)
    tabs = _rope_tables(seq)
    for l in range(ffn1_norm.shape[0]):
        x = _layer(x, l, nb, seq, tabs, p)
    return x[:bp * seq].reshape(bp, seq, D), x[bp * seq:].reshape(bs, seq, D)
```

```python
import functools
import math

import jax
import jax.numpy as jnp
from jax import lax
from jax.experimental import pallas as pl
from jax.experimental.pallas import tpu as pltpu

F32 = jnp.float32
BF16 = jnp.bfloat16

RMS_EPS = 1e-6
ROPE_THETA = 10000.0
LOG2E = math.log2(math.e)
LANES = 128
V7X_VMEM_BYTES = 64 * 1024 * 1024
VMEM_LIMIT = V7X_VMEM_BYTES - 8 * 1024 * 1024

A_HEADS, A_DH = 8, 128
A_HW = 2 * A_DH
A_W = A_HEADS * A_HW
B_GROUPS = ((128, 1), (512, 4), (2048, 16))
B_HPG, B_DH = 6, 128
B_HEADS = B_HPG * len(B_GROUPS)
B_W = B_HEADS * B_DH
B_GW = B_HPG * B_DH
C_HEADS, C_Q_RANK, C_KV_RANK, C_NOPE, C_ROPE, C_VDIM = 16, 1024, 512, 128, 64, 128
C_DQK = C_NOPE + C_ROPE
C_HPAD = 2 * LANES
C_OUT = C_HEADS * C_VDIM
AB_COLS = 3 * A_W + 3 * B_W
C_COLS = C_Q_RANK + C_KV_RANK + C_ROPE
C_COLS_PAD = 1792
NEG_BIG = -1e30
ATT_TK = 512


def _tile(n, t):
    t = min(n, t)
    assert n % t == 0, (n, t)
    return t


def _params(sem):
    return pltpu.CompilerParams(dimension_semantics=sem, vmem_limit_bytes=VMEM_LIMIT)


def _row(a):
    return a.reshape(1, -1).astype(F32)


def _rmsnorm_kernel(x_ref, g_ref, o_ref):
    x = x_ref[...]
    y = x * lax.rsqrt(jnp.mean(x * x, axis=-1, keepdims=True) + RMS_EPS)
    o_ref[...] = (y * g_ref[...]).astype(o_ref.dtype)


def _rmsnorm(x, g, tm=512):
    T, D = x.shape
    tm = _tile(T, tm)
    return pl.pallas_call(
        _rmsnorm_kernel,
        grid=(T // tm,),
        in_specs=[pl.BlockSpec((tm, D), lambda i: (i, 0)), pl.BlockSpec((1, D), lambda i: (0, 0))],
        out_specs=pl.BlockSpec((tm, D), lambda i: (i, 0)),
        out_shape=jax.ShapeDtypeStruct((T, D), BF16),
        compiler_params=_params(("parallel",)),
        name="rmsnorm",
    )(x, _row(g))


def _ffn_up_kernel(h_ref, wg_ref, wu_ref, o_ref):
    h = h_ref[...]
    g = jnp.dot(h, wg_ref[...], preferred_element_type=F32)
    u = jnp.dot(h, wu_ref[...], preferred_element_type=F32)
    o_ref[...] = (g * jax.nn.sigmoid(g) * u).astype(o_ref.dtype)


def _ffn_up(h, wg, wu, tm=1024, tn=512):
    T, D = h.shape
    N = wg.shape[1]
    tm, tn = _tile(T, tm), _tile(N, tn)
    return pl.pallas_call(
        _ffn_up_kernel,
        grid=(T // tm, N // tn),
        in_specs=[pl.BlockSpec((tm, D), lambda i, j: (i, 0)),
                  pl.BlockSpec((D, tn), lambda i, j: (0, j)),
                  pl.BlockSpec((D, tn), lambda i, j: (0, j))],
        out_specs=pl.BlockSpec((tm, tn), lambda i, j: (i, j)),
        out_shape=jax.ShapeDtypeStruct((T, N), BF16),
        compiler_params=_params(("parallel", "parallel")),
        name="ffn_up",
    )(h, wg, wu)


def _mm_res_kernel(a_ref, b_ref, x_ref, o_ref, *, scale, nk):
    k = pl.program_id(2)
    part = jnp.dot(a_ref[...], b_ref[...], preferred_element_type=F32)

    @pl.when(k == 0)
    def _():
        o_ref[...] = part

    @pl.when(k > 0)
    def _():
        o_ref[...] += part

    @pl.when(k == nk - 1)
    def _():
        o_ref[...] = x_ref[...] + scale * o_ref[...]


def _mm_res(a, b, x, scale, tm=1024, tn=1024, tk=1024):
    T, K = a.shape
    N = b.shape[1]
    tm, tn, tk = _tile(T, tm), _tile(N, tn), _tile(K, tk)
    nk = K // tk
    return pl.pallas_call(
        functools.partial(_mm_res_kernel, scale=scale, nk=nk),
        grid=(T // tm, N // tn, nk),
        in_specs=[pl.BlockSpec((tm, tk), lambda i, j, k: (i, k)),
                  pl.BlockSpec((tk, tn), lambda i, j, k: (k, j)),
                  pl.BlockSpec((tm, tn), lambda i, j, k: (i, j))],
        out_specs=pl.BlockSpec((tm, tn), lambda i, j, k: (i, j)),
        out_shape=jax.ShapeDtypeStruct((T, N), F32),
        compiler_params=_params(("parallel", "parallel", "arbitrary")),
        name="mm_residual",
    )(a, b, x)


def _rope128(y, cos, sin_signed):
    return y * cos + pltpu.roll(y, LANES // 2, 1) * sin_signed


def _norm_rope128(a, g, cos, sin_signed):
    return _rope128(a * lax.rsqrt(jnp.mean(a * a, axis=-1, keepdims=True) + RMS_EPS) * g, cos, sin_signed)


def _proj_rope_kernel(u_ref, w_ref, g_ref, cos_ref, sin_ref, o_ref, *, row_chunks):
    tm = u_ref.shape[0] // row_chunks
    for r in range(row_chunks):
        rows = slice(r * tm, (r + 1) * tm)
        acc = jnp.dot(u_ref[rows, :], w_ref[...], preferred_element_type=F32)
        for c in range(acc.shape[1] // LANES):
            sl = slice(c * LANES, (c + 1) * LANES)
            o_ref[rows, sl] = _norm_rope128(acc[:, sl], g_ref[:, sl], cos_ref[rows, :],
                                            sin_ref[rows, :]).astype(o_ref.dtype)


def _proj_rope(u, w, gains, cos, sin, seq, tm=1024, tn=256):
    T, D = u.shape
    N = w.shape[1]
    tm, tn = _tile(seq, tm), _tile(N, tn)
    npos = seq // tm
    return pl.pallas_call(
        functools.partial(_proj_rope_kernel, row_chunks=4 if tm % 1024 == 0 else 1),
        grid=(T // tm, N // tn),
        in_specs=[pl.BlockSpec((tm, D), lambda i, j: (i, 0)),
                  pl.BlockSpec((D, tn), lambda i, j: (0, j)),
                  pl.BlockSpec((1, tn), lambda i, j: (0, j)),
                  pl.BlockSpec((tm, LANES), lambda i, j: (i % npos, 0)),
                  pl.BlockSpec((tm, LANES), lambda i, j: (i % npos, 0))],
        out_specs=pl.BlockSpec((tm, tn), lambda i, j: (i, j)),
        out_shape=jax.ShapeDtypeStruct((T, N), BF16),
        compiler_params=_params(("parallel", "parallel")),
        name="proj_rope",
    )(u, w, gains, cos, sin)


def _proj_vt_kernel(u_ref, w_ref, o_ref):
    acc = jnp.dot(u_ref[...], w_ref[...], preferred_element_type=F32)
    tk = o_ref.shape[3]
    for c in range(o_ref.shape[1]):
        o_ref[0, c] = acc[c * tk:(c + 1) * tk, :].T.astype(o_ref.dtype)


def _proj_vt(u, w, heads, tk, tm=1024):
    T, D = u.shape
    hw = w.shape[1] // heads
    tm = _tile(T, tm)
    tk = _tile(tm, tk)
    return pl.pallas_call(
        _proj_vt_kernel,
        grid=(T // tm, heads),
        in_specs=[pl.BlockSpec((tm, D), lambda i, h: (i, 0)), pl.BlockSpec((D, hw), lambda i, h: (0, h))],
        out_specs=pl.BlockSpec((1, tm // tk, hw, tk), lambda i, h: (h, i, 0, 0)),
        out_shape=jax.ShapeDtypeStruct((heads, T // tk, hw, tk), BF16),
        compiler_params=_params(("parallel", "parallel")),
        name="proj_vt",
    )(u, w)


def _proj_b_kernel(u_ref, w_ref, g_ref, cos_ref, sin_ref, o_ref, scr, *, dil):
    t = pl.program_id(1)
    heads = [slice(h * B_DH, (h + 1) * B_DH) for h in range(B_HPG)]
    tm = u_ref.shape[0]
    row_chunks = 4 if tm % 1024 == 0 else 1
    chunks = [slice(r * (tm // row_chunks), (r + 1) * (tm // row_chunks)) for r in range(row_chunks)]

    @pl.when(t < 2)
    def _():
        for rows in chunks:
            acc = jnp.dot(u_ref[rows, :], w_ref[...], preferred_element_type=F32)
            for h, sl in enumerate(heads):
                scr[h, rows, :] = _norm_rope128(acc[:, sl], g_ref[:, sl], cos_ref[rows, :], sin_ref[rows, :])

    @pl.when(t == 2)
    def _():
        for rows in chunks:
            acc = jnp.dot(u_ref[rows, :], w_ref[...], preferred_element_type=F32)
            for h, sl in enumerate(heads):
                scr[h, rows, :] = acc[:, sl]

    n = tm // dil
    for c in range(dil):
        for h, sl in enumerate(heads):
            o_ref[c, :, sl] = scr[h, pl.ds(c, n, stride=dil), :].astype(o_ref.dtype)


def _proj_b(u, w_b, gains, cos, sin, g, nb, seq, tm=1024):
    T, D = u.shape
    dil = B_GROUPS[g][1]
    tm = _tile(seq, tm)
    assert tm % dil == 0 and (tm // dil) % 16 == 0
    npos = seq // tm
    ng = len(B_GROUPS)
    return pl.pallas_call(
        functools.partial(_proj_b_kernel, dil=dil),
        grid=(T // tm, 3),
        in_specs=[pl.BlockSpec((tm, D), lambda i, t: (i, 0)),
                  pl.BlockSpec((D, B_GW), lambda i, t: (0, t * ng + g)),
                  pl.BlockSpec((1, B_GW), lambda i, t: (0, t)),
                  pl.BlockSpec((tm, LANES), lambda i, t: (i % npos, 0)),
                  pl.BlockSpec((tm, LANES), lambda i, t: (i % npos, 0))],
        out_specs=pl.BlockSpec((dil, tm // dil, B_GW), lambda i, t: (i // npos, i % npos, t)),
        out_shape=jax.ShapeDtypeStruct((nb * dil, seq // dil, 3 * B_GW), BF16),
        scratch_shapes=[pltpu.VMEM((B_HPG, tm, B_DH), F32)],
        compiler_params=_params(("parallel", "arbitrary")),
        name=f"proj_b_d{dil}",
    )(u, w_b, gains, cos, sin)


def _mm_kernel(a_ref, b_ref, o_ref):
    o_ref[...] = jnp.dot(a_ref[...], b_ref[...], preferred_element_type=F32).astype(o_ref.dtype)


def _mm(a, b, out_dtype, tm=1024, tn=256):
    T, K = a.shape
    N = b.shape[1]
    tm, tn = _tile(T, tm), _tile(N, tn)
    return pl.pallas_call(
        _mm_kernel,
        grid=(T // tm, N // tn),
        in_specs=[pl.BlockSpec((tm, K), lambda i, j: (i, 0)), pl.BlockSpec((K, tn), lambda i, j: (0, j))],
        out_specs=pl.BlockSpec((tm, tn), lambda i, j: (i, j)),
        out_shape=jax.ShapeDtypeStruct((T, N), out_dtype),
        compiler_params=_params(("parallel", "parallel")),
        name="mm",
    )(a, b)


def _rope64_in128(y, cos, sin_lo, sin_hi):
    q = C_ROPE // 2
    return y * cos + pltpu.roll(y, LANES - q, 1) * sin_lo + pltpu.roll(y, q, 1) * sin_hi


def _mla_q_kernel(cq_ref, ng_ref, w_ref, g_ref, cos_ref, slo_ref, shi_ref, o_ref, a_scr, *, tn):
    @pl.when(pl.program_id(1) == 0)
    def _():
        x = cq_ref[...]
        y = x * lax.rsqrt(jnp.mean(x * x, axis=-1, keepdims=True) + RMS_EPS) * ng_ref[...]
        a_scr[...] = y.astype(a_scr.dtype)

    tm = a_scr.shape[0]
    row_chunks = 2 if tm % 512 == 0 else 1
    for rc in range(row_chunks):
        rows = slice(rc * (tm // row_chunks), (rc + 1) * (tm // row_chunks))
        acc = jnp.dot(a_scr[rows, :], w_ref[...], preferred_element_type=F32)
        for h in range(tn // C_HPAD):
            a = acc[:, h * C_HPAD:(h + 1) * C_HPAD]
            r = lax.rsqrt(jnp.sum(a * a, axis=-1, keepdims=True) * (1.0 / C_DQK) + RMS_EPS)
            y = a * r * g_ref[...]
            o_ref[rows, h * C_HPAD:h * C_HPAD + LANES] = y[:, :LANES].astype(o_ref.dtype)
            o_ref[rows, h * C_HPAD + LANES:(h + 1) * C_HPAD] = _rope64_in128(
                y[:, LANES:], cos_ref[rows, :], slo_ref[rows, :], shi_ref[rows, :]).astype(o_ref.dtype)


def _mla_q(pc, qa_gain, wq_pad, q_gain_pad, cos, slo, shi, seq, tm=512, tn=512):
    T = pc.shape[0]
    N = wq_pad.shape[1]
    tm, tn = _tile(seq, tm), _tile(N, tn)
    npos = seq // tm
    pos = lambda i, j: (i % npos, 0)
    return pl.pallas_call(
        functools.partial(_mla_q_kernel, tn=tn),
        grid=(T // tm, N // tn),
        in_specs=[pl.BlockSpec((tm, C_Q_RANK), lambda i, j: (i, 0)),
                  pl.BlockSpec((1, C_Q_RANK), lambda i, j: (0, 0)),
                  pl.BlockSpec((C_Q_RANK, tn), lambda i, j: (0, j)),
                  pl.BlockSpec((1, C_HPAD), lambda i, j: (0, 0)),
                  pl.BlockSpec((tm, LANES), pos), pl.BlockSpec((tm, LANES), pos), pl.BlockSpec((tm, LANES), pos)],
        out_specs=pl.BlockSpec((tm, tn), lambda i, j: (i, j)),
        out_shape=jax.ShapeDtypeStruct((T, N), BF16),
        scratch_shapes=[pltpu.VMEM((tm, C_Q_RANK), BF16)],
        compiler_params=_params(("parallel", "arbitrary")),
        name="mla_q_up",
    )(pc, qa_gain, wq_pad, q_gain_pad, cos, slo, shi)


def _mla_kv_kernel(ckv_ref, ng_ref, kr_ref, w_ref, gn_ref, gr_ref, cos_ref, slo_ref, shi_ref,
                   k_ref, vt_ref, a_scr, *, heads):
    @pl.when(pl.program_id(1) == 0)
    def _():
        x = ckv_ref[...]
        y = x * lax.rsqrt(jnp.mean(x * x, axis=-1, keepdims=True) + RMS_EPS) * ng_ref[...]
        a_scr[...] = y.astype(a_scr.dtype)

    acc = jnp.dot(a_scr[...], w_ref[...], preferred_element_type=F32)
    kr = kr_ref[...]
    kr_ss = jnp.sum(kr * kr, axis=-1, keepdims=True)
    tk = vt_ref.shape[3]
    for h in range(heads):
        base = h * (C_NOPE + C_VDIM)
        kn = acc[:, base:base + C_NOPE]
        r = lax.rsqrt((jnp.sum(kn * kn, axis=-1, keepdims=True) + kr_ss) * (1.0 / C_DQK) + RMS_EPS)
        k_ref[:, h * C_HPAD:h * C_HPAD + LANES] = (kn * r * gn_ref[...]).astype(k_ref.dtype)
        k_ref[:, h * C_HPAD + LANES:(h + 1) * C_HPAD] = _rope64_in128(
            kr * r * gr_ref[...], cos_ref[...], slo_ref[...], shi_ref[...]).astype(k_ref.dtype)
        vt = acc[:, base + C_NOPE:base + C_NOPE + C_VDIM].T
        for c in range(vt_ref.shape[1]):
            vt_ref[h, c] = vt[:, c * tk:(c + 1) * tk].astype(vt_ref.dtype)


def _mla_kv(pc, kva_gain, wkv, k_gain_nope, k_gain_rope, cos, slo, shi, seq, tk, tm=512, heads=2):
    T = pc.shape[0]
    tm = _tile(seq, tm)
    tk = _tile(tm, tk)
    npos = seq // tm
    pos = lambda i, j: (i % npos, 0)
    hw = C_NOPE + C_VDIM
    return pl.pallas_call(
        functools.partial(_mla_kv_kernel, heads=heads),
        grid=(T // tm, C_HEADS // heads),
        in_specs=[pl.BlockSpec((tm, C_KV_RANK), lambda i, j: (i, C_Q_RANK // C_KV_RANK)),
                  pl.BlockSpec((1, C_KV_RANK), lambda i, j: (0, 0)),
                  pl.BlockSpec((tm, LANES), lambda i, j: (i, (C_Q_RANK + C_KV_RANK) // LANES)),
                  pl.BlockSpec((C_KV_RANK, heads * hw), lambda i, j: (0, j)),
                  pl.BlockSpec((1, LANES), lambda i, j: (0, 0)),
                  pl.BlockSpec((1, LANES), lambda i, j: (0, 0)),
                  pl.BlockSpec((tm, LANES), pos), pl.BlockSpec((tm, LANES), pos), pl.BlockSpec((tm, LANES), pos)],
        out_specs=[pl.BlockSpec((tm, heads * C_HPAD), lambda i, j: (i, j)),
                   pl.BlockSpec((heads, tm // tk, C_VDIM, tk), lambda i, j: (j, i, 0, 0))],
        out_shape=[jax.ShapeDtypeStruct((T, C_HEADS * C_HPAD), BF16),
                   jax.ShapeDtypeStruct((C_HEADS, T // tk, C_VDIM, tk), BF16)],
        scratch_shapes=[pltpu.VMEM((tm, C_KV_RANK), BF16)],
        compiler_params=_params(("parallel", "arbitrary")),
        name="mla_kv_up",
    )(pc, kva_gain, pc, wkv, k_gain_nope, k_gain_rope, cos, slo, shi)


def _nt_dot(a, b):
    return lax.dot_general(a, b, (((1,), (1,)), ((), ())), preferred_element_type=F32)


def _scores_t(k, q, s_ref):
    s = _nt_dot(k, q)
    s_ref[...] = s
    return jnp.max(s, axis=0, keepdims=True)


def _softmax_step_t(s_ref, s_max, vt, m_prev, l_prev, acc_ref, idx):
    m_new = jnp.maximum(m_prev, s_max)
    alpha = jnp.exp2(m_prev - m_new)
    p = jnp.exp2(s_ref[...] - m_new)
    l_new = alpha * l_prev + jnp.sum(p, axis=0, keepdims=True)
    acc_ref[idx] = acc_ref[idx] * alpha + jnp.dot(vt, p.astype(vt.dtype), preferred_element_type=F32)
    return m_new, l_new


def _pipelined_key_loop(nk, scores, consume, init):
    assert nk % 2 == 0

    def pair(jj, carry):
        state, smax0 = carry
        j = 2 * jj
        smax1 = scores(1, j + 1)
        state = consume(0, j, smax0, state)
        smax0 = scores(0, j + 2)
        state = consume(1, j + 1, smax1, state)
        return state, smax0

    state, smax0 = lax.fori_loop(0, nk // 2 - 1, pair, (init, scores(0, 0)))
    smax1 = scores(1, nk - 1)
    state = consume(0, nk - 2, smax0, state)
    return consume(1, nk - 1, smax1, state)


def _diff_attn_kernel(q_ref, k_ref, vt_ref, lq1_ref, lk1_ref, lq2_ref, lk2_ref, g_ref, o_ref, acc_ref, s_ref,
                      *, lam_init):
    tq = q_ref.shape[0]
    tk = vt_ref.shape[3]
    nk = k_ref.shape[0] // tk
    q = q_ref[...]
    acc_ref[...] = jnp.zeros_like(acc_ref)
    subs = [slice(sub * A_DH, (sub + 1) * A_DH) for sub in range(2)]

    def scores(slot, j):
        k = k_ref[pl.ds(pl.multiple_of(j * tk, tk), tk), :]
        return tuple(_scores_t(k[:, sl], q[:, sl], s_ref.at[slot, sub]) for sub, sl in enumerate(subs))

    def consume(slot, j, smax, state):
        vt = vt_ref[0, j]
        m0, l0, m1, l1 = state
        m0, l0 = _softmax_step_t(s_ref.at[slot, 0], smax[0], vt, m0, l0, acc_ref, 0)
        m1, l1 = _softmax_step_t(s_ref.at[slot, 1], smax[1], vt, m1, l1, acc_ref, 1)
        return m0, l0, m1, l1

    neg = jnp.full((1, tq), -jnp.inf, F32)
    zero = jnp.zeros((1, tq), F32)
    _, l0, _, l1 = _pipelined_key_loop(nk, scores, consume, (neg, zero, neg, zero))

    lam = (jnp.exp(jnp.sum(lq1_ref[...] * lk1_ref[...], axis=-1, keepdims=True))
           - jnp.exp(jnp.sum(lq2_ref[...] * lk2_ref[...], axis=-1, keepdims=True)) + lam_init)
    o = (acc_ref[0] * (1.0 / l0) - lam * (acc_ref[1] * (1.0 / l1))).T
    y = o * lax.rsqrt(jnp.mean(o * o, axis=-1, keepdims=True) + RMS_EPS) * g_ref[...]
    o_ref[...] = (y * (1.0 - lam_init)).astype(o_ref.dtype)


def _diff_attn(qk, vt, lq1, lk1, lq2, lk2, subln, nb, seq, lam_init, tq=1024):
    T = qk.shape[0]
    tk = vt.shape[3]
    tq = _tile(seq, tq)
    nq = seq // tq
    small = pl.BlockSpec((1, A_DH), lambda b, h, i: (0, 0))
    return pl.pallas_call(
        functools.partial(_diff_attn_kernel, lam_init=lam_init),
        grid=(nb, A_HEADS, nq),
        in_specs=[pl.BlockSpec((tq, A_HW), lambda b, h, i: (b * nq + i, h)),
                  pl.BlockSpec((seq, A_HW), lambda b, h, i: (b, A_HEADS + h)),
                  pl.BlockSpec((1, seq // tk, A_HW, tk), lambda b, h, i: (h, b, 0, 0)),
                  small, small, small, small,
                  pl.BlockSpec((1, A_HW), lambda b, h, i: (0, 0))],
        out_specs=pl.BlockSpec((tq, A_HW), lambda b, h, i: (b * nq + i, h)),
        out_shape=jax.ShapeDtypeStruct((T, A_W), BF16),
        scratch_shapes=[pltpu.VMEM((2, A_HW, tq), F32), pltpu.VMEM((2, 2, tk, tq), F32)],
        compiler_params=_params(("parallel", "parallel", "parallel")),
        name="diff_attn",
    )(qk, qk, vt, _row(lq1), _row(lk1), _row(lq2), _row(lk2), _row(subln))


def _mla_attn_kernel(q_ref, k_ref, vt_ref, o_ref, acc_ref, s_ref):
    tq = q_ref.shape[0]
    tk = vt_ref.shape[3]
    nk = k_ref.shape[0] // tk
    q = q_ref[...]
    acc_ref[...] = jnp.zeros_like(acc_ref)

    def scores(slot, j):
        return _scores_t(k_ref[pl.ds(pl.multiple_of(j * tk, tk), tk), :], q, s_ref.at[slot])

    def consume(slot, j, smax, state):
        return _softmax_step_t(s_ref.at[slot], smax, vt_ref[0, j], state[0], state[1], acc_ref, 0)

    init = (jnp.full((1, tq), -jnp.inf, F32), jnp.zeros((1, tq), F32))
    _, l = _pipelined_key_loop(nk, scores, consume, init)
    o_ref[...] = (acc_ref[0] * (1.0 / l)).T.astype(o_ref.dtype)


def _mla_attn(qc, kc, vt, nb, seq, tq=1024):
    T = qc.shape[0]
    tk = vt.shape[3]
    tq = _tile(seq, tq)
    nq = seq // tq
    return pl.pallas_call(
        _mla_attn_kernel,
        grid=(nb, C_HEADS, nq),
        in_specs=[pl.BlockSpec((tq, C_HPAD), lambda b, h, i: (b * nq + i, h)),
                  pl.BlockSpec((seq, C_HPAD), lambda b, h, i: (b, h)),
                  pl.BlockSpec((1, seq // tk, C_VDIM, tk), lambda b, h, i: (h, b, 0, 0))],
        out_specs=pl.BlockSpec((tq, C_VDIM), lambda b, h, i: (b * nq + i, h)),
        out_shape=jax.ShapeDtypeStruct((T, C_OUT), BF16),
        scratch_shapes=[pltpu.VMEM((1, C_VDIM, tq), F32), pltpu.VMEM((2, tk, tq), F32)],
        compiler_params=_params(("parallel", "parallel", "parallel")),
        name="mla_attn",
    )(qc, kc, vt)


def _window_attn_kernel(q_ref, kp_ref, km_ref, kn_ref, vp_ref, vm_ref, vn_ref, o_ref, lse_ref, *, half, sub_len):
    tq = q_ref.shape[0]
    i = pl.program_id(1)
    span = tq + 2 * half
    qpos = i * tq + lax.broadcasted_iota(jnp.int32, (tq, span), 0)
    kpos = i * tq - half + lax.broadcasted_iota(jnp.int32, (tq, span), 1)
    valid = (jnp.abs(kpos - qpos) <= half) & (kpos >= 0) & (kpos < sub_len)
    scale = B_DH ** -0.5
    for h in range(B_HPG):
        sl = slice(h * B_DH, (h + 1) * B_DH)
        k = jnp.concatenate([kp_ref[:, sl], km_ref[:, sl], kn_ref[:, sl]], axis=0)
        v = jnp.concatenate([vp_ref[:, sl], vm_ref[:, sl], vn_ref[:, sl]], axis=0)
        s = jnp.where(valid, _nt_dot(q_ref[:, sl], k) * scale, NEG_BIG)
        m = jnp.max(s, axis=-1, keepdims=True)
        p = jnp.exp(s - m)
        l = jnp.sum(p, axis=-1, keepdims=True)
        o = jnp.dot(p.astype(v.dtype), v, preferred_element_type=F32) / l
        o_ref[:, sl] = o.astype(o_ref.dtype)
        lse_ref[:, sl] = jnp.broadcast_to(m + jnp.log(l), (tq, B_DH))


def _window_attn(qkv, g, tq=256):
    win, dil = B_GROUPS[g]
    half = win // (2 * dil)
    nsub, L, _ = qkv.shape
    tq = _tile(L, tq)
    assert tq % half == 0
    nq = L // tq
    r = tq // half
    nh = L // half
    flat = qkv.reshape(nsub * L, 3 * B_GW)

    def main(col):
        return pl.BlockSpec((tq, B_GW), lambda s, i: (s * nq + i, col))

    def prev(col):
        return pl.BlockSpec((half, B_GW), lambda s, i: (s * nh + jnp.maximum(i * r - 1, 0), col))

    def nxt(col):
        return pl.BlockSpec((half, B_GW), lambda s, i: (s * nh + jnp.minimum((i + 1) * r, nh - 1), col))

    out_spec = pl.BlockSpec((tq, B_GW), lambda s, i: (s * nq + i, 0))
    o, lse = pl.pallas_call(
        functools.partial(_window_attn_kernel, half=half, sub_len=L),
        grid=(nsub, nq),
        in_specs=[main(0), prev(1), main(1), nxt(1), prev(2), main(2), nxt(2)],
        out_specs=[out_spec, out_spec],
        out_shape=[jax.ShapeDtypeStruct((nsub * L, B_GW), BF16), jax.ShapeDtypeStruct((nsub * L, B_GW), F32)],
        compiler_params=_params(("parallel", "parallel")),
        name=f"window_attn_d{dil}",
    )(flat, flat, flat, flat, flat, flat, flat)
    return o.reshape(nsub, L, B_GW), lse.reshape(nsub, L, B_GW)


def _combine_kernel(o0, o1, o2, l0, l1, l2, o_ref, so, sl):
    tm = o_ref.shape[0]
    heads = [slice(h * B_DH, (h + 1) * B_DH) for h in range(B_HPG)]
    for g, (og, lg) in enumerate(((o0, l0), (o1, l1), (o2, l2))):
        dil = og.shape[0]
        n = tm // dil
        for c in range(dil):
            for h, hs in enumerate(heads):
                so[g * B_HPG + h, pl.ds(c, n, stride=dil), :] = og[c, :, hs].astype(F32)
                sl[g * B_HPG + h, pl.ds(c, n, stride=dil), :] = lg[c, :, hs]
    for h, hs in enumerate(heads):
        a0, a1, a2 = sl[h], sl[B_HPG + h], sl[2 * B_HPG + h]
        m = jnp.maximum(jnp.maximum(a0, a1), a2)
        e0, e1, e2 = jnp.exp(a0 - m), jnp.exp(a1 - m), jnp.exp(a2 - m)
        num = e0 * so[h] + e1 * so[B_HPG + h] + e2 * so[2 * B_HPG + h]
        o_ref[:, hs] = (num / (e0 + e1 + e2)).astype(o_ref.dtype)


def _combine(outs, lses, nb, seq, tm=512):
    tm = _tile(seq, tm)
    npos = seq // tm
    ng = len(outs)

    def spec(a):
        dil = a.shape[0] // nb
        return pl.BlockSpec((dil, tm // dil, B_GW), lambda i: (i // npos, i % npos, 0))

    return pl.pallas_call(
        _combine_kernel,
        grid=(nb * npos,),
        in_specs=[spec(a) for a in outs] + [spec(a) for a in lses],
        out_specs=pl.BlockSpec((tm, B_GW), lambda i: (i, 0)),
        out_shape=jax.ShapeDtypeStruct((nb * seq, B_GW), BF16),
        scratch_shapes=[pltpu.VMEM((ng * B_HPG, tm, B_DH), F32), pltpu.VMEM((ng * B_HPG, tm, B_DH), F32)],
        compiler_params=_params(("parallel",)),
        name="window_combine",
    )(*outs, *lses)


def _merge_kernel(u_ref, oa_ref, ob_ref, oc_ref, wga_ref, wgb_ref, wgc_ref, wa_ref, wb_ref, wc_ref, o_ref):
    u = u_ref[...]

    def branch(wg_ref, x_ref, w_ref):
        gate = jax.nn.sigmoid(jnp.dot(u, wg_ref[...], preferred_element_type=F32))
        return gate * jnp.dot(x_ref[...], w_ref[...], preferred_element_type=F32)

    o_ref[...] = (branch(wga_ref, oa_ref, wa_ref) + branch(wgb_ref, ob_ref, wb_ref)
                  + branch(wgc_ref, oc_ref, wc_ref)).astype(o_ref.dtype)


def _merge(u, oa, ob, oc, w_gates, wa, wb, wc, tm=512, tn=256):
    T, D = u.shape
    tm, tn = _tile(T, tm), _tile(D, tn)
    nj = D // tn
    row = lambda w: pl.BlockSpec((tm, w), lambda i, j: (i, 0))
    col = lambda k: pl.BlockSpec((k, tn), lambda i, j: (0, j))
    gate = lambda n: pl.BlockSpec((D, tn), lambda i, j: (0, n * nj + j))
    return pl.pallas_call(
        _merge_kernel,
        grid=(T // tm, nj),
        in_specs=[row(D), row(oa.shape[1]), row(ob.shape[1]), row(oc.shape[1]),
                  gate(0), gate(1), gate(2), col(wa.shape[0]), col(wb.shape[0]), col(wc.shape[0])],
        out_specs=pl.BlockSpec((tm, tn), lambda i, j: (i, j)),
        out_shape=jax.ShapeDtypeStruct((T, D), BF16),
        compiler_params=_params(("parallel", "parallel")),
        name="gated_merge",
    )(u, oa, ob, oc, w_gates, w_gates, w_gates, wa, wb, wc)


def _rope_tables(seq):
    pos = jnp.arange(seq, dtype=F32)[:, None]

    def ang(d):
        half = d // 2
        inv = 1.0 / (ROPE_THETA ** (jnp.arange(half, dtype=F32) * (2.0 / d)))
        return pos * inv[None, :]

    a = ang(A_DH)
    cos128 = jnp.concatenate([jnp.cos(a), jnp.cos(a)], axis=-1)
    sin128 = jnp.concatenate([-jnp.sin(a), jnp.sin(a)], axis=-1)
    c = ang(C_ROPE)
    z = jnp.zeros_like(c)
    cos64 = jnp.concatenate([jnp.cos(c), jnp.cos(c), z, z], axis=-1)
    sin_lo = jnp.concatenate([-jnp.sin(c), z, z, z], axis=-1)
    sin_hi = jnp.concatenate([z, jnp.sin(c), z, z], axis=-1)
    return cos128, sin128, cos64, sin_lo, sin_hi


def _pad_cols(w, n):
    return jnp.pad(w, ((0, 0), (0, n - w.shape[1])))


def _layer(x, l, nb, seq, tabs, p):
    cos128, sin128, cos64, sin_lo, sin_hi = tabs
    D = x.shape[1]
    ff = p["ffn1_wg"].shape[2]
    ffp = -(-ff // 1024) * 1024

    def ffn(x, norm, wg, wu, wd):
        h = _rmsnorm(x, norm[l])
        act = _ffn_up(h, _pad_cols(wg[l], ffp).astype(BF16), _pad_cols(wu[l], ffp).astype(BF16))
        wd_p = jnp.pad(wd[l], ((0, ffp - ff), (0, 0))).astype(BF16)
        return _mm_res(act, wd_p, x, 0.5, tk=ffp // 4)

    x = ffn(x, p["ffn1_norm"], p["ffn1_wg"], p["ffn1_wu"], p["ffn1_wd"])

    u = _rmsnorm(x, p["mix_norm"][l])
    w_in = p["w_in"][l]
    b0 = 3 * A_W
    tile_h = lambda g, n: jnp.tile(g, n)

    a_gains = jnp.concatenate([tile_h(p["a_qnorm"][l] * (A_DH ** -0.5 * LOG2E), A_W // A_DH),
                               tile_h(p["a_knorm"][l], A_W // A_DH)]).reshape(1, 2 * A_W)
    a_qk = _proj_rope(u, w_in[:, :2 * A_W].astype(BF16), a_gains, cos128, sin128, seq)
    a_vt = _proj_vt(u, w_in[:, 2 * A_W:b0].astype(BF16), A_HEADS, ATT_TK)
    lam_init = 0.8 - 0.6 * math.exp(-0.3 * l)
    oa = _diff_attn(a_qk, a_vt, p["a_lq1"][l], p["a_lk1"][l], p["a_lq2"][l], p["a_lk2"][l], p["a_subln"][l],
                    nb, seq, lam_init)

    b_gains = jnp.concatenate([tile_h(p["b_qnorm"][l], B_HPG), tile_h(p["b_knorm"][l], B_HPG),
                               jnp.ones((B_GW,), F32)]).reshape(1, 3 * B_GW)
    w_b = w_in[:, b0:AB_COLS].astype(BF16)
    wres = [_window_attn(_proj_b(u, w_b, b_gains, cos128, sin128, g, nb, seq), g) for g in range(len(B_GROUPS))]
    ob = _combine([r[0] for r in wres], [r[1] for r in wres], nb, seq)

    pc = _mm(u, _pad_cols(w_in[:, AB_COLS:AB_COLS + C_COLS], C_COLS_PAD).astype(BF16), F32)
    wq = p["c_wq_up"][l].reshape(C_Q_RANK, C_HEADS, C_DQK)
    wq_pad = jnp.pad(wq, ((0, 0), (0, 0), (0, C_HPAD - C_DQK))).reshape(C_Q_RANK, C_HEADS * C_HPAD).astype(BF16)
    qg = p["c_qnorm"][l] * (C_DQK ** -0.5 * LOG2E)
    kg = p["c_knorm"][l]
    pad_to = lambda v, n: jnp.pad(v, (0, n - v.shape[0]))
    qc = _mla_q(pc, _row(p["c_qa_norm"][l]), wq_pad, _row(pad_to(qg, C_HPAD)), cos64, sin_lo, sin_hi, seq)
    kc, c_vt = _mla_kv(pc, _row(p["c_kva_norm"][l]), p["c_wkv_up"][l].astype(BF16), _row(kg[:C_NOPE]),
                       _row(pad_to(kg[C_NOPE:], LANES)), cos64, sin_lo, sin_hi, seq, ATT_TK)
    oc = _mla_attn(qc, kc, c_vt, nb, seq)

    m = _merge(u, oa, ob, oc, w_in[:, AB_COLS + C_COLS:].astype(BF16), p["w_br_a"][l].astype(BF16),
               p["w_br_b"][l].astype(BF16), p["w_br_c"][l].astype(BF16))
    x = _mm_res(m, p["w_out"][l].astype(BF16), x, 1.0, tm=1024, tn=512, tk=D)

    return ffn(x, p["ffn2_norm"], p["ffn2_wg"], p["ffn2_wu"], p["ffn2_wd"])


def kernel(x_prompt, x_sample, ffn1_norm, ffn1_wg, ffn1_wu, ffn1_wd, mix_norm, w_in, a_qnorm, a_knorm, a_lq1, a_lk1, a_lq2, a_lk2, a_subln, b_qnorm, b_knorm, c_qa_norm, c_kva_norm, c_wq_up, c_wkv_up, c_qnorm, c_knorm, w_br_a, w_br_b, w_br_c, w_out, ffn2_norm, ffn2_wg, ffn2_wu, ffn2_wd):
    p = dict(ffn1_norm=ffn1_norm, ffn1_wg=ffn1_wg, ffn1_wu=ffn1_wu, ffn1_wd=ffn1_wd, mix_norm=mix_norm, w_in=w_in,
             a_qnorm=a_qnorm, a_knorm=a_knorm, a_lq1=a_lq1, a_lk1=a_lk1, a_lq2=a_lq2, a_lk2=a_lk2, a_subln=a_subln,
             b_qnorm=b_qnorm, b_knorm=b_knorm, c_qa_norm=c_qa_norm, c_kva_norm=c_kva_norm, c_wq_up=c_wq_up,
             c_wkv_up=c_wkv_up, c_qnorm=c_qnorm, c_knorm=c_knorm, w_br_a=w_br_a, w_br_b=w_br_b, w_br_c=w_br_c,
             w_out=w_out, ffn2_norm=ffn2_norm, ffn2_wg=ffn2_wg, ffn2_wu=ffn2_wu, ffn2_wd=ffn2_wd)
    bp, seq, D = x_prompt.shape
    bs, seq_s, _ = x_sample.shape
    assert seq == seq_s
    nb = bp + bs
    x = jnp.concatenate([x_prompt.reshape(bp * seq, D), x_sample.reshape(bs * seq, D)], axis=0)
    tabs = _rope_tables(seq)
    for l in range(ffn1_norm.shape[0]):
        x = _layer(x, l, nb, seq, tabs, p)
    return x[:bp * seq].reshape(bp, seq, D), x[bp * seq:].reshape(bs, seq, D)
```

```python
import functools
import math

import jax
import jax.numpy as jnp
from jax import lax
from jax.experimental import pallas as pl
from jax.experimental.pallas import tpu as pltpu

F32 = jnp.float32
BF16 = jnp.bfloat16

RMS_EPS = 1e-6
ROPE_THETA = 10000.0
LOG2E = math.log2(math.e)
LANES = 128
V7X_VMEM_BYTES = 64 * 1024 * 1024
VMEM_LIMIT = V7X_VMEM_BYTES - 8 * 1024 * 1024

A_HEADS, A_DH = 8, 128
A_HW = 2 * A_DH
A_W = A_HEADS * A_HW
B_GROUPS = ((128, 1), (512, 4), (2048, 16))
B_HPG, B_DH = 6, 128
B_HEADS = B_HPG * len(B_GROUPS)
B_W = B_HEADS * B_DH
B_GW = B_HPG * B_DH
C_HEADS, C_Q_RANK, C_KV_RANK, C_NOPE, C_ROPE, C_VDIM = 16, 1024, 512, 128, 64, 128
C_DQK = C_NOPE + C_ROPE
C_HPAD = 2 * LANES
C_OUT = C_HEADS * C_VDIM
N_BRANCH = 3
AB_COLS = 3 * A_W + 3 * B_W
C_COLS = C_Q_RANK + C_KV_RANK + C_ROPE
C_COLS_PAD = 1792
NEG_BIG = -1e30
ATT_TK = 512


def _tile(n, t):
    t = min(n, t)
    assert n % t == 0, (n, t)
    return t


def _params(sem):
    return pltpu.CompilerParams(dimension_semantics=sem, vmem_limit_bytes=VMEM_LIMIT)


def _row(a):
    return a.reshape(1, -1).astype(F32)


def _rmsnorm_kernel(x_ref, g_ref, o_ref):
    x = x_ref[...]
    y = x * lax.rsqrt(jnp.mean(x * x, axis=-1, keepdims=True) + RMS_EPS)
    o_ref[...] = (y * g_ref[...]).astype(o_ref.dtype)


def _rmsnorm(x, g, tm=512):
    T, D = x.shape
    tm = _tile(T, tm)
    return pl.pallas_call(
        _rmsnorm_kernel,
        grid=(T // tm,),
        in_specs=[pl.BlockSpec((tm, D), lambda i: (i, 0)), pl.BlockSpec((1, D), lambda i: (0, 0))],
        out_specs=pl.BlockSpec((tm, D), lambda i: (i, 0)),
        out_shape=jax.ShapeDtypeStruct((T, D), BF16),
        compiler_params=_params(("parallel",)),
        name="rmsnorm",
    )(x, _row(g))


def _cast_kernel(x_ref, o_ref, *, rows, cols):
    x = x_ref[...]
    tr, tc = x.shape
    if rows is not None:
        r = pl.program_id(0) * tr + lax.broadcasted_iota(jnp.int32, (tr, tc), 0)
        x = jnp.where(r < rows, x, 0.0)
    if cols is not None:
        c = pl.program_id(1) * tc + lax.broadcasted_iota(jnp.int32, (tr, tc), 1)
        x = jnp.where(c < cols, x, 0.0)
    o_ref[...] = x.astype(o_ref.dtype)


def _cast(w, l, col0=0, ncols=None, rows_out=None, cols_out=None, tr=2048, tc=512):
    _, R, C = w.shape
    ncols = C - col0 if ncols is None else ncols
    rows_out = R if rows_out is None else rows_out
    cols_out = ncols if cols_out is None else cols_out
    tr, tc = min(tr, rows_out), min(tc, cols_out)
    assert col0 % tc == 0
    c0 = col0 // tc
    gr, gc = pl.cdiv(rows_out, tr), pl.cdiv(cols_out, tc)
    rows = R if gr * tr > R else None
    cols = ncols if (gc * tc > ncols and cols_out > ncols) else None
    return pl.pallas_call(
        functools.partial(_cast_kernel, rows=rows, cols=cols),
        grid=(gr, gc),
        in_specs=[pl.BlockSpec((None, tr, tc), lambda i, j: (l, i, j + c0))],
        out_specs=pl.BlockSpec((tr, tc), lambda i, j: (i, j)),
        out_shape=jax.ShapeDtypeStruct((rows_out, cols_out), BF16),
        compiler_params=_params(("parallel", "parallel")),
        name="cast_bf16",
    )(w)


def _cast_shift_kernel(a_ref, b_ref, o_ref, *, shift):
    tc = a_ref.shape[1]
    lane = lax.broadcasted_iota(jnp.int32, a_ref.shape, 1)
    x = jnp.where(lane < tc - shift, pltpu.roll(a_ref[...], tc - shift, 1), pltpu.roll(b_ref[...], tc - shift, 1))
    o_ref[...] = x.astype(o_ref.dtype)


def _cast_shifted(w, l, col0, ncols, tr=2048, tc=256):
    _, R, C = w.shape
    tr = min(tr, R)
    shift = col0 % tc
    c0 = col0 // tc
    assert ncols % tc == 0 and R % tr == 0 and shift > 0
    return pl.pallas_call(
        functools.partial(_cast_shift_kernel, shift=shift),
        grid=(R // tr, ncols // tc),
        in_specs=[pl.BlockSpec((None, tr, tc), lambda i, j: (l, i, j + c0)),
                  pl.BlockSpec((None, tr, tc), lambda i, j: (l, i, j + c0 + 1))],
        out_specs=pl.BlockSpec((tr, tc), lambda i, j: (i, j)),
        out_shape=jax.ShapeDtypeStruct((R, ncols), BF16),
        compiler_params=_params(("parallel", "parallel")),
        name="cast_bf16_shifted",
    )(w, w)


def _ffn_up_kernel(h_ref, wg_ref, wu_ref, o_ref):
    h = h_ref[...]
    g = jnp.dot(h, wg_ref[...], preferred_element_type=F32)
    u = jnp.dot(h, wu_ref[...], preferred_element_type=F32)
    o_ref[...] = (g * jax.nn.sigmoid(g) * u).astype(o_ref.dtype)


def _ffn_up(h, wg, wu, tm=1024, tn=512):
    T, D = h.shape
    N = wg.shape[1]
    tm, tn = _tile(T, tm), _tile(N, tn)
    return pl.pallas_call(
        _ffn_up_kernel,
        grid=(T // tm, N // tn),
        in_specs=[pl.BlockSpec((tm, D), lambda i, j: (i, 0)),
                  pl.BlockSpec((D, tn), lambda i, j: (0, j)),
                  pl.BlockSpec((D, tn), lambda i, j: (0, j))],
        out_specs=pl.BlockSpec((tm, tn), lambda i, j: (i, j)),
        out_shape=jax.ShapeDtypeStruct((T, N), BF16),
        compiler_params=_params(("parallel", "parallel")),
        name="ffn_up",
    )(h, wg, wu)


def _mm_res_kernel(a_ref, b_ref, x_ref, o_ref, *, scale, nk):
    k = pl.program_id(2)
    part = jnp.dot(a_ref[...], b_ref[...], preferred_element_type=F32)

    @pl.when(k == 0)
    def _():
        o_ref[...] = part

    @pl.when(k > 0)
    def _():
        o_ref[...] += part

    @pl.when(k == nk - 1)
    def _():
        o_ref[...] = x_ref[...] + scale * o_ref[...]


def _mm_res(a, b, x, scale, tm=1024, tn=1024, tk=1024):
    T, K = a.shape
    N = b.shape[1]
    tm, tn, tk = _tile(T, tm), _tile(N, tn), _tile(K, tk)
    nk = K // tk
    return pl.pallas_call(
        functools.partial(_mm_res_kernel, scale=scale, nk=nk),
        grid=(T // tm, N // tn, nk),
        in_specs=[pl.BlockSpec((tm, tk), lambda i, j, k: (i, k)),
                  pl.BlockSpec((tk, tn), lambda i, j, k: (k, j)),
                  pl.BlockSpec((tm, tn), lambda i, j, k: (i, j))],
        out_specs=pl.BlockSpec((tm, tn), lambda i, j, k: (i, j)),
        out_shape=jax.ShapeDtypeStruct((T, N), F32),
        compiler_params=_params(("parallel", "parallel", "arbitrary")),
        name="mm_residual",
    )(a, b, x)


def _rope128(y, cos, sin_signed):
    return y * cos + pltpu.roll(y, LANES // 2, 1) * sin_signed


def _norm_rope128(a, g, cos, sin_signed):
    return _rope128(a * lax.rsqrt(jnp.mean(a * a, axis=-1, keepdims=True) + RMS_EPS) * g, cos, sin_signed)


def _proj_rope_kernel(u_ref, w_ref, g_ref, cos_ref, sin_ref, o_ref, *, row_chunks):
    tm = u_ref.shape[0] // row_chunks
    for r in range(row_chunks):
        rows = slice(r * tm, (r + 1) * tm)
        acc = jnp.dot(u_ref[rows, :], w_ref[...], preferred_element_type=F32)
        for c in range(acc.shape[1] // LANES):
            sl = slice(c * LANES, (c + 1) * LANES)
            o_ref[rows, sl] = _norm_rope128(acc[:, sl], g_ref[:, sl], cos_ref[rows, :],
                                            sin_ref[rows, :]).astype(o_ref.dtype)


def _proj_rope(u, w, gains, cos, sin, seq, tm=1024, tn=256):
    T, D = u.shape
    N = w.shape[1]
    tm, tn = _tile(seq, tm), _tile(N, tn)
    npos = seq // tm
    return pl.pallas_call(
        functools.partial(_proj_rope_kernel, row_chunks=4 if tm % 1024 == 0 else 1),
        grid=(T // tm, N // tn),
        in_specs=[pl.BlockSpec((tm, D), lambda i, j: (i, 0)),
                  pl.BlockSpec((D, tn), lambda i, j: (0, j)),
                  pl.BlockSpec((1, tn), lambda i, j: (0, j)),
                  pl.BlockSpec((tm, LANES), lambda i, j: (i % npos, 0)),
                  pl.BlockSpec((tm, LANES), lambda i, j: (i % npos, 0))],
        out_specs=pl.BlockSpec((tm, tn), lambda i, j: (i, j)),
        out_shape=jax.ShapeDtypeStruct((T, N), BF16),
        compiler_params=_params(("parallel", "parallel")),
        name="proj_rope",
    )(u, w, gains, cos, sin)


def _proj_vt_kernel(u_ref, w_ref, o_ref):
    acc = jnp.dot(u_ref[...], w_ref[...], preferred_element_type=F32)
    tk = o_ref.shape[3]
    for c in range(o_ref.shape[1]):
        o_ref[0, c] = acc[c * tk:(c + 1) * tk, :].T.astype(o_ref.dtype)


def _proj_vt(u, w, heads, tk, tm=1024):
    T, D = u.shape
    hw = w.shape[1] // heads
    tm = _tile(T, tm)
    tk = _tile(tm, tk)
    return pl.pallas_call(
        _proj_vt_kernel,
        grid=(T // tm, heads),
        in_specs=[pl.BlockSpec((tm, D), lambda i, h: (i, 0)), pl.BlockSpec((D, hw), lambda i, h: (0, h))],
        out_specs=pl.BlockSpec((1, tm // tk, hw, tk), lambda i, h: (h, i, 0, 0)),
        out_shape=jax.ShapeDtypeStruct((heads, T // tk, hw, tk), BF16),
        compiler_params=_params(("parallel", "parallel")),
        name="proj_vt",
    )(u, w)


def _proj_b_kernel(u_ref, w_ref, g_ref, cos_ref, sin_ref, o_ref, scr, *, dil):
    t = pl.program_id(1)
    heads = [slice(h * B_DH, (h + 1) * B_DH) for h in range(B_HPG)]
    tm = u_ref.shape[0]
    row_chunks = 4 if tm % 1024 == 0 else 1
    chunks = [slice(r * (tm // row_chunks), (r + 1) * (tm // row_chunks)) for r in range(row_chunks)]

    @pl.when(t < 2)
    def _():
        for rows in chunks:
            acc = jnp.dot(u_ref[rows, :], w_ref[...], preferred_element_type=F32)
            for h, sl in enumerate(heads):
                scr[h, rows, :] = _norm_rope128(acc[:, sl], g_ref[:, sl], cos_ref[rows, :], sin_ref[rows, :])

    @pl.when(t == 2)
    def _():
        for rows in chunks:
            acc = jnp.dot(u_ref[rows, :], w_ref[...], preferred_element_type=F32)
            for h, sl in enumerate(heads):
                scr[h, rows, :] = acc[:, sl]

    n = tm // dil
    for c in range(dil):
        for h, sl in enumerate(heads):
            o_ref[c, :, sl] = scr[h, pl.ds(c, n, stride=dil), :].astype(o_ref.dtype)


def _proj_b(u, w_b, gains, cos, sin, g, nb, seq, tm=1024):
    T, D = u.shape
    dil = B_GROUPS[g][1]
    tm = _tile(seq, tm)
    assert tm % dil == 0 and (tm // dil) % 16 == 0
    npos = seq // tm
    ng = len(B_GROUPS)
    return pl.pallas_call(
        functools.partial(_proj_b_kernel, dil=dil),
        grid=(T // tm, 3),
        in_specs=[pl.BlockSpec((tm, D), lambda i, t: (i, 0)),
                  pl.BlockSpec((D, B_GW), lambda i, t: (0, t * ng + g)),
                  pl.BlockSpec((1, B_GW), lambda i, t: (0, t)),
                  pl.BlockSpec((tm, LANES), lambda i, t: (i % npos, 0)),
                  pl.BlockSpec((tm, LANES), lambda i, t: (i % npos, 0))],
        out_specs=pl.BlockSpec((dil, tm // dil, B_GW), lambda i, t: (i // npos, i % npos, t)),
        out_shape=jax.ShapeDtypeStruct((nb * dil, seq // dil, 3 * B_GW), BF16),
        scratch_shapes=[pltpu.VMEM((B_HPG, tm, B_DH), F32)],
        compiler_params=_params(("parallel", "arbitrary")),
        name=f"proj_b_d{dil}",
    )(u, w_b, gains, cos, sin)


def _mm_kernel(a_ref, b_ref, o_ref):
    o_ref[...] = jnp.dot(a_ref[...], b_ref[...], preferred_element_type=F32).astype(o_ref.dtype)


def _mm(a, b, out_dtype, tm=1024, tn=256):
    T, K = a.shape
    N = b.shape[1]
    tm, tn = _tile(T, tm), _tile(N, tn)
    return pl.pallas_call(
        _mm_kernel,
        grid=(T // tm, N // tn),
        in_specs=[pl.BlockSpec((tm, K), lambda i, j: (i, 0)), pl.BlockSpec((K, tn), lambda i, j: (0, j))],
        out_specs=pl.BlockSpec((tm, tn), lambda i, j: (i, j)),
        out_shape=jax.ShapeDtypeStruct((T, N), out_dtype),
        compiler_params=_params(("parallel", "parallel")),
        name="mm",
    )(a, b)


def _rope64_in128(y, cos, sin_lo, sin_hi):
    q = C_ROPE // 2
    return y * cos + pltpu.roll(y, LANES - q, 1) * sin_lo + pltpu.roll(y, q, 1) * sin_hi


def _mla_q_kernel(cq_ref, ng_ref, w_ref, g_ref, cos_ref, slo_ref, shi_ref, o_ref, a_scr, *, tn):
    @pl.when(pl.program_id(1) == 0)
    def _():
        x = cq_ref[...]
        y = x * lax.rsqrt(jnp.mean(x * x, axis=-1, keepdims=True) + RMS_EPS) * ng_ref[...]
        a_scr[...] = y.astype(a_scr.dtype)

    tm = a_scr.shape[0]
    row_chunks = 2 if tm % 512 == 0 else 1
    for rc in range(row_chunks):
        rows = slice(rc * (tm // row_chunks), (rc + 1) * (tm // row_chunks))
        acc = jnp.dot(a_scr[rows, :], w_ref[...], preferred_element_type=F32)
        for h in range(tn // C_HPAD):
            a = acc[:, h * C_HPAD:(h + 1) * C_HPAD]
            r = lax.rsqrt(jnp.sum(a * a, axis=-1, keepdims=True) * (1.0 / C_DQK) + RMS_EPS)
            y = a * r * g_ref[...]
            o_ref[rows, h * C_HPAD:h * C_HPAD + LANES] = y[:, :LANES].astype(o_ref.dtype)
            o_ref[rows, h * C_HPAD + LANES:(h + 1) * C_HPAD] = _rope64_in128(
                y[:, LANES:], cos_ref[rows, :], slo_ref[rows, :], shi_ref[rows, :]).astype(o_ref.dtype)


def _mla_q(pc, qa_gain, wq_pad, q_gain_pad, cos, slo, shi, seq, tm=512, tn=512):
    T = pc.shape[0]
    N = wq_pad.shape[1]
    tm, tn = _tile(seq, tm), _tile(N, tn)
    npos = seq // tm
    pos = lambda i, j: (i % npos, 0)
    return pl.pallas_call(
        functools.partial(_mla_q_kernel, tn=tn),
        grid=(T // tm, N // tn),
        in_specs=[pl.BlockSpec((tm, C_Q_RANK), lambda i, j: (i, 0)),
                  pl.BlockSpec((1, C_Q_RANK), lambda i, j: (0, 0)),
                  pl.BlockSpec((C_Q_RANK, tn), lambda i, j: (0, j)),
                  pl.BlockSpec((1, C_HPAD), lambda i, j: (0, 0)),
                  pl.BlockSpec((tm, LANES), pos), pl.BlockSpec((tm, LANES), pos), pl.BlockSpec((tm, LANES), pos)],
        out_specs=pl.BlockSpec((tm, tn), lambda i, j: (i, j)),
        out_shape=jax.ShapeDtypeStruct((T, N), BF16),
        scratch_shapes=[pltpu.VMEM((tm, C_Q_RANK), BF16)],
        compiler_params=_params(("parallel", "arbitrary")),
        name="mla_q_up",
    )(pc, qa_gain, wq_pad, q_gain_pad, cos, slo, shi)


def _mla_kv_kernel(ckv_ref, ng_ref, kr_ref, w_ref, gn_ref, gr_ref, cos_ref, slo_ref, shi_ref,
                   k_ref, vt_ref, a_scr, *, heads):
    @pl.when(pl.program_id(1) == 0)
    def _():
        x = ckv_ref[...]
        y = x * lax.rsqrt(jnp.mean(x * x, axis=-1, keepdims=True) + RMS_EPS) * ng_ref[...]
        a_scr[...] = y.astype(a_scr.dtype)

    acc = jnp.dot(a_scr[...], w_ref[...], preferred_element_type=F32)
    kr = kr_ref[...]
    kr_ss = jnp.sum(kr * kr, axis=-1, keepdims=True)
    tk = vt_ref.shape[3]
    for h in range(heads):
        base = h * (C_NOPE + C_VDIM)
        kn = acc[:, base:base + C_NOPE]
        r = lax.rsqrt((jnp.sum(kn * kn, axis=-1, keepdims=True) + kr_ss) * (1.0 / C_DQK) + RMS_EPS)
        k_ref[:, h * C_HPAD:h * C_HPAD + LANES] = (kn * r * gn_ref[...]).astype(k_ref.dtype)
        k_ref[:, h * C_HPAD + LANES:(h + 1) * C_HPAD] = _rope64_in128(
            kr * r * gr_ref[...], cos_ref[...], slo_ref[...], shi_ref[...]).astype(k_ref.dtype)
        vt = acc[:, base + C_NOPE:base + C_NOPE + C_VDIM].T
        for c in range(vt_ref.shape[1]):
            vt_ref[h, c] = vt[:, c * tk:(c + 1) * tk].astype(vt_ref.dtype)


def _mla_kv(pc, kva_gain, wkv, k_gain_nope, k_gain_rope, cos, slo, shi, seq, tk, tm=512, heads=2):
    T = pc.shape[0]
    tm = _tile(seq, tm)
    tk = _tile(tm, tk)
    npos = seq // tm
    pos = lambda i, j: (i % npos, 0)
    hw = C_NOPE + C_VDIM
    return pl.pallas_call(
        functools.partial(_mla_kv_kernel, heads=heads),
        grid=(T // tm, C_HEADS // heads),
        in_specs=[pl.BlockSpec((tm, C_KV_RANK), lambda i, j: (i, C_Q_RANK // C_KV_RANK)),
                  pl.BlockSpec((1, C_KV_RANK), lambda i, j: (0, 0)),
                  pl.BlockSpec((tm, LANES), lambda i, j: (i, (C_Q_RANK + C_KV_RANK) // LANES)),
                  pl.BlockSpec((C_KV_RANK, heads * hw), lambda i, j: (0, j)),
                  pl.BlockSpec((1, LANES), lambda i, j: (0, 0)),
                  pl.BlockSpec((1, LANES), lambda i, j: (0, 0)),
                  pl.BlockSpec((tm, LANES), pos), pl.BlockSpec((tm, LANES), pos), pl.BlockSpec((tm, LANES), pos)],
        out_specs=[pl.BlockSpec((tm, heads * C_HPAD), lambda i, j: (i, j)),
                   pl.BlockSpec((heads, tm // tk, C_VDIM, tk), lambda i, j: (j, i, 0, 0))],
        out_shape=[jax.ShapeDtypeStruct((T, C_HEADS * C_HPAD), BF16),
                   jax.ShapeDtypeStruct((C_HEADS, T // tk, C_VDIM, tk), BF16)],
        scratch_shapes=[pltpu.VMEM((tm, C_KV_RANK), BF16)],
        compiler_params=_params(("parallel", "arbitrary")),
        name="mla_kv_up",
    )(pc, kva_gain, pc, wkv, k_gain_nope, k_gain_rope, cos, slo, shi)


def _nt_dot(a, b):
    return lax.dot_general(a, b, (((1,), (1,)), ((), ())), preferred_element_type=F32)


def _scores_t(k, q, s_ref):
    s = _nt_dot(k, q)
    s_ref[...] = s
    return jnp.max(s, axis=0, keepdims=True)


def _softmax_step_t(s_ref, s_max, vt, m_prev, l_prev, acc_ref, idx):
    m_new = jnp.maximum(m_prev, s_max)
    alpha = jnp.exp2(m_prev - m_new)
    p = jnp.exp2(s_ref[...] - m_new)
    l_new = alpha * l_prev + jnp.sum(p, axis=0, keepdims=True)
    acc_ref[idx] = acc_ref[idx] * alpha + jnp.dot(vt, p.astype(vt.dtype), preferred_element_type=F32)
    return m_new, l_new


def _pipelined_key_loop(nk, scores, consume, init):
    assert nk % 2 == 0

    def pair(jj, carry):
        state, smax0 = carry
        j = 2 * jj
        smax1 = scores(1, j + 1)
        state = consume(0, j, smax0, state)
        smax0 = scores(0, j + 2)
        state = consume(1, j + 1, smax1, state)
        return state, smax0

    state, smax0 = lax.fori_loop(0, nk // 2 - 1, pair, (init, scores(0, 0)))
    smax1 = scores(1, nk - 1)
    state = consume(0, nk - 2, smax0, state)
    return consume(1, nk - 1, smax1, state)


def _diff_attn_kernel(q_ref, k_ref, vt_ref, lq1_ref, lk1_ref, lq2_ref, lk2_ref, g_ref, o_ref, acc_ref, s_ref,
                      *, lam_init):
    tq = q_ref.shape[0]
    tk = vt_ref.shape[3]
    nk = k_ref.shape[0] // tk
    q = q_ref[...]
    acc_ref[...] = jnp.zeros_like(acc_ref)
    subs = [slice(sub * A_DH, (sub + 1) * A_DH) for sub in range(2)]

    def scores(slot, j):
        k = k_ref[pl.ds(pl.multiple_of(j * tk, tk), tk), :]
        return tuple(_scores_t(k[:, sl], q[:, sl], s_ref.at[slot, sub]) for sub, sl in enumerate(subs))

    def consume(slot, j, smax, state):
        vt = vt_ref[0, j]
        m0, l0, m1, l1 = state
        m0, l0 = _softmax_step_t(s_ref.at[slot, 0], smax[0], vt, m0, l0, acc_ref, 0)
        m1, l1 = _softmax_step_t(s_ref.at[slot, 1], smax[1], vt, m1, l1, acc_ref, 1)
        return m0, l0, m1, l1

    neg = jnp.full((1, tq), -jnp.inf, F32)
    zero = jnp.zeros((1, tq), F32)
    _, l0, _, l1 = _pipelined_key_loop(nk, scores, consume, (neg, zero, neg, zero))

    lam = (jnp.exp(jnp.sum(lq1_ref[...] * lk1_ref[...], axis=-1, keepdims=True))
           - jnp.exp(jnp.sum(lq2_ref[...] * lk2_ref[...], axis=-1, keepdims=True)) + lam_init)
    o = (acc_ref[0] * (1.0 / l0) - lam * (acc_ref[1] * (1.0 / l1))).T
    y = o * lax.rsqrt(jnp.mean(o * o, axis=-1, keepdims=True) + RMS_EPS) * g_ref[...]
    o_ref[...] = (y * (1.0 - lam_init)).astype(o_ref.dtype)


def _diff_attn(qk, vt, lq1, lk1, lq2, lk2, subln, nb, seq, lam_init, tq=1024):
    T = qk.shape[0]
    tk = vt.shape[3]
    tq = _tile(seq, tq)
    nq = seq // tq
    small = pl.BlockSpec((1, A_DH), lambda b, h, i: (0, 0))
    return pl.pallas_call(
        functools.partial(_diff_attn_kernel, lam_init=lam_init),
        grid=(nb, A_HEADS, nq),
        in_specs=[pl.BlockSpec((tq, A_HW), lambda b, h, i: (b * nq + i, h)),
                  pl.BlockSpec((seq, A_HW), lambda b, h, i: (b, A_HEADS + h)),
                  pl.BlockSpec((1, seq // tk, A_HW, tk), lambda b, h, i: (h, b, 0, 0)),
                  small, small, small, small,
                  pl.BlockSpec((1, A_HW), lambda b, h, i: (0, 0))],
        out_specs=pl.BlockSpec((tq, A_HW), lambda b, h, i: (b * nq + i, h)),
        out_shape=jax.ShapeDtypeStruct((T, A_W), BF16),
        scratch_shapes=[pltpu.VMEM((2, A_HW, tq), F32), pltpu.VMEM((2, 2, tk, tq), F32)],
        compiler_params=_params(("parallel", "parallel", "parallel")),
        name="diff_attn",
    )(qk, qk, vt, _row(lq1), _row(lk1), _row(lq2), _row(lk2), _row(subln))


def _mla_attn_kernel(q_ref, k_ref, vt_ref, o_ref, acc_ref, s_ref):
    tq = q_ref.shape[0]
    tk = vt_ref.shape[3]
    nk = k_ref.shape[0] // tk
    q = q_ref[...]
    acc_ref[...] = jnp.zeros_like(acc_ref)

    def scores(slot, j):
        return _scores_t(k_ref[pl.ds(pl.multiple_of(j * tk, tk), tk), :], q, s_ref.at[slot])

    def consume(slot, j, smax, state):
        return _softmax_step_t(s_ref.at[slot], smax, vt_ref[0, j], state[0], state[1], acc_ref, 0)

    init = (jnp.full((1, tq), -jnp.inf, F32), jnp.zeros((1, tq), F32))
    _, l = _pipelined_key_loop(nk, scores, consume, init)
    o_ref[...] = (acc_ref[0] * (1.0 / l)).T.astype(o_ref.dtype)


def _mla_attn(qc, kc, vt, nb, seq, tq=1024):
    T = qc.shape[0]
    tk = vt.shape[3]
    tq = _tile(seq, tq)
    nq = seq // tq
    return pl.pallas_call(
        _mla_attn_kernel,
        grid=(nb, C_HEADS, nq),
        in_specs=[pl.BlockSpec((tq, C_HPAD), lambda b, h, i: (b * nq + i, h)),
                  pl.BlockSpec((seq, C_HPAD), lambda b, h, i: (b, h)),
                  pl.BlockSpec((1, seq // tk, C_VDIM, tk), lambda b, h, i: (h, b, 0, 0))],
        out_specs=pl.BlockSpec((tq, C_VDIM), lambda b, h, i: (b * nq + i, h)),
        out_shape=jax.ShapeDtypeStruct((T, C_OUT), BF16),
        scratch_shapes=[pltpu.VMEM((1, C_VDIM, tq), F32), pltpu.VMEM((2, tk, tq), F32)],
        compiler_params=_params(("parallel", "parallel", "parallel")),
        name="mla_attn",
    )(qc, kc, vt)


def _window_attn_kernel(q_ref, kp_ref, km_ref, kn_ref, vp_ref, vm_ref, vn_ref, o_ref, lse_ref, *, half, sub_len):
    tq = q_ref.shape[0]
    i = pl.program_id(1)
    span = tq + 2 * half
    qpos = i * tq + lax.broadcasted_iota(jnp.int32, (tq, span), 0)
    kpos = i * tq - half + lax.broadcasted_iota(jnp.int32, (tq, span), 1)
    valid = (jnp.abs(kpos - qpos) <= half) & (kpos >= 0) & (kpos < sub_len)
    scale = B_DH ** -0.5
    for h in range(B_HPG):
        sl = slice(h * B_DH, (h + 1) * B_DH)
        k = jnp.concatenate([kp_ref[:, sl], km_ref[:, sl], kn_ref[:, sl]], axis=0)
        v = jnp.concatenate([vp_ref[:, sl], vm_ref[:, sl], vn_ref[:, sl]], axis=0)
        s = jnp.where(valid, _nt_dot(q_ref[:, sl], k) * scale, NEG_BIG)
        m = jnp.max(s, axis=-1, keepdims=True)
        p = jnp.exp(s - m)
        l = jnp.sum(p, axis=-1, keepdims=True)
        o = jnp.dot(p.astype(v.dtype), v, preferred_element_type=F32) / l
        o_ref[:, sl] = o.astype(o_ref.dtype)
        lse_ref[:, sl] = jnp.broadcast_to(m + jnp.log(l), (tq, B_DH))


def _window_attn(qkv, g, tq=256):
    win, dil = B_GROUPS[g]
    half = win // (2 * dil)
    nsub, L, _ = qkv.shape
    tq = _tile(L, tq)
    assert tq % half == 0
    nq = L // tq
    r = tq // half
    nh = L // half
    flat = qkv.reshape(nsub * L, 3 * B_GW)

    def main(col):
        return pl.BlockSpec((tq, B_GW), lambda s, i: (s * nq + i, col))

    def prev(col):
        return pl.BlockSpec((half, B_GW), lambda s, i: (s * nh + jnp.maximum(i * r - 1, 0), col))

    def nxt(col):
        return pl.BlockSpec((half, B_GW), lambda s, i: (s * nh + jnp.minimum((i + 1) * r, nh - 1), col))

    out_spec = pl.BlockSpec((tq, B_GW), lambda s, i: (s * nq + i, 0))
    o, lse = pl.pallas_call(
        functools.partial(_window_attn_kernel, half=half, sub_len=L),
        grid=(nsub, nq),
        in_specs=[main(0), prev(1), main(1), nxt(1), prev(2), main(2), nxt(2)],
        out_specs=[out_spec, out_spec],
        out_shape=[jax.ShapeDtypeStruct((nsub * L, B_GW), BF16), jax.ShapeDtypeStruct((nsub * L, B_GW), F32)],
        compiler_params=_params(("parallel", "parallel")),
        name=f"window_attn_d{dil}",
    )(flat, flat, flat, flat, flat, flat, flat)
    return o.reshape(nsub, L, B_GW), lse.reshape(nsub, L, B_GW)


def _combine_kernel(o0, o1, o2, l0, l1, l2, o_ref, so, sl):
    tm = o_ref.shape[0]
    heads = [slice(h * B_DH, (h + 1) * B_DH) for h in range(B_HPG)]
    for g, (og, lg) in enumerate(((o0, l0), (o1, l1), (o2, l2))):
        dil = og.shape[0]
        n = tm // dil
        for c in range(dil):
            for h, hs in enumerate(heads):
                so[g * B_HPG + h, pl.ds(c, n, stride=dil), :] = og[c, :, hs].astype(F32)
                sl[g * B_HPG + h, pl.ds(c, n, stride=dil), :] = lg[c, :, hs]
    for h, hs in enumerate(heads):
        a0, a1, a2 = sl[h], sl[B_HPG + h], sl[2 * B_HPG + h]
        m = jnp.maximum(jnp.maximum(a0, a1), a2)
        e0, e1, e2 = jnp.exp(a0 - m), jnp.exp(a1 - m), jnp.exp(a2 - m)
        num = e0 * so[h] + e1 * so[B_HPG + h] + e2 * so[2 * B_HPG + h]
        o_ref[:, hs] = (num / (e0 + e1 + e2)).astype(o_ref.dtype)


def _combine(outs, lses, nb, seq, tm=512):
    tm = _tile(seq, tm)
    npos = seq // tm
    ng = len(outs)

    def spec(a):
        dil = a.shape[0] // nb
        return pl.BlockSpec((dil, tm // dil, B_GW), lambda i: (i // npos, i % npos, 0))

    return pl.pallas_call(
        _combine_kernel,
        grid=(nb * npos,),
        in_specs=[spec(a) for a in outs] + [spec(a) for a in lses],
        out_specs=pl.BlockSpec((tm, B_GW), lambda i: (i, 0)),
        out_shape=jax.ShapeDtypeStruct((nb * seq, B_GW), BF16),
        scratch_shapes=[pltpu.VMEM((ng * B_HPG, tm, B_DH), F32), pltpu.VMEM((ng * B_HPG, tm, B_DH), F32)],
        compiler_params=_params(("parallel",)),
        name="window_combine",
    )(*outs, *lses)


def _merge_kernel(u_ref, oa_ref, ob_ref, oc_ref, wga_ref, wgb_ref, wgc_ref, wa_ref, wb_ref, wc_ref, o_ref):
    u = u_ref[...]

    def branch(wg_ref, x_ref, w_ref):
        gate = jax.nn.sigmoid(jnp.dot(u, wg_ref[...], preferred_element_type=F32))
        return gate * jnp.dot(x_ref[...], w_ref[...], preferred_element_type=F32)

    o_ref[...] = (branch(wga_ref, oa_ref, wa_ref) + branch(wgb_ref, ob_ref, wb_ref)
                  + branch(wgc_ref, oc_ref, wc_ref)).astype(o_ref.dtype)


def _merge(u, oa, ob, oc, w_gates, wa, wb, wc, tm=512, tn=256):
    T, D = u.shape
    tm, tn = _tile(T, tm), _tile(D, tn)
    nj = D // tn
    row = lambda w: pl.BlockSpec((tm, w), lambda i, j: (i, 0))
    col = lambda k: pl.BlockSpec((k, tn), lambda i, j: (0, j))
    gate = lambda n: pl.BlockSpec((D, tn), lambda i, j: (0, n * nj + j))
    return pl.pallas_call(
        _merge_kernel,
        grid=(T // tm, nj),
        in_specs=[row(D), row(oa.shape[1]), row(ob.shape[1]), row(oc.shape[1]),
                  gate(0), gate(1), gate(2), col(wa.shape[0]), col(wb.shape[0]), col(wc.shape[0])],
        out_specs=pl.BlockSpec((tm, tn), lambda i, j: (i, j)),
        out_shape=jax.ShapeDtypeStruct((T, D), BF16),
        compiler_params=_params(("parallel", "parallel")),
        name="gated_merge",
    )(u, oa, ob, oc, w_gates, w_gates, w_gates, wa, wb, wc)


def _rope_tables(seq):
    pos = jnp.arange(seq, dtype=F32)[:, None]

    def ang(d):
        half = d // 2
        inv = 1.0 / (ROPE_THETA ** (jnp.arange(half, dtype=F32) * (2.0 / d)))
        return pos * inv[None, :]

    a = ang(A_DH)
    cos128 = jnp.concatenate([jnp.cos(a), jnp.cos(a)], axis=-1)
    sin128 = jnp.concatenate([-jnp.sin(a), jnp.sin(a)], axis=-1)
    c = ang(C_ROPE)
    z = jnp.zeros_like(c)
    cos64 = jnp.concatenate([jnp.cos(c), jnp.cos(c), z, z], axis=-1)
    sin_lo = jnp.concatenate([-jnp.sin(c), z, z, z], axis=-1)
    sin_hi = jnp.concatenate([z, jnp.sin(c), z, z], axis=-1)
    return cos128, sin128, cos64, sin_lo, sin_hi


def _layer(x, l, nb, seq, tabs, p):
    cos128, sin128, cos64, sin_lo, sin_hi = tabs
    D = x.shape[1]
    ff = p["ffn1_wg"].shape[2]
    ffp = -(-ff // 1024) * 1024

    def ffn(x, norm, wg, wu, wd):
        h = _rmsnorm(x, norm[l])
        act = _ffn_up(h, _cast(wg, l, cols_out=ffp), _cast(wu, l, cols_out=ffp))
        return _mm_res(act, _cast(wd, l, rows_out=ffp), x, 0.5, tk=ffp // 4)

    x = ffn(x, p["ffn1_norm"], p["ffn1_wg"], p["ffn1_wu"], p["ffn1_wd"])

    u = _rmsnorm(x, p["mix_norm"][l])
    w_in = p["w_in"]
    b0 = 3 * A_W
    tile_h = lambda g, n: jnp.tile(g, n)

    a_gains = jnp.concatenate([tile_h(p["a_qnorm"][l] * (A_DH ** -0.5 * LOG2E), A_W // A_DH),
                               tile_h(p["a_knorm"][l], A_W // A_DH)]).reshape(1, 2 * A_W)
    a_qk = _proj_rope(u, _cast(w_in, l, 0, 2 * A_W), a_gains, cos128, sin128, seq)
    a_vt = _proj_vt(u, _cast(w_in, l, 2 * A_W, A_W), A_HEADS, ATT_TK)
    lam_init = 0.8 - 0.6 * math.exp(-0.3 * l)
    oa = _diff_attn(a_qk, a_vt, p["a_lq1"][l], p["a_lk1"][l], p["a_lq2"][l], p["a_lk2"][l], p["a_subln"][l],
                    nb, seq, lam_init)

    b_gains = jnp.concatenate([tile_h(p["b_qnorm"][l], B_HPG), tile_h(p["b_knorm"][l], B_HPG),
                               jnp.ones((B_GW,), F32)]).reshape(1, 3 * B_GW)
    w_b = _cast(w_in, l, b0, 3 * B_W)
    wres = [_window_attn(_proj_b(u, w_b, b_gains, cos128, sin128, g, nb, seq), g) for g in range(len(B_GROUPS))]
    ob = _combine([r[0] for r in wres], [r[1] for r in wres], nb, seq)

    pc = _mm(u, _cast(w_in, l, AB_COLS, C_COLS, cols_out=C_COLS_PAD, tc=256), F32)
    wq = p["c_wq_up"][l].reshape(C_Q_RANK, C_HEADS, C_DQK)
    wq_pad = jnp.pad(wq, ((0, 0), (0, 0), (0, C_HPAD - C_DQK))).reshape(C_Q_RANK, C_HEADS * C_HPAD).astype(BF16)
    qg = p["c_qnorm"][l] * (C_DQK ** -0.5 * LOG2E)
    kg = p["c_knorm"][l]
    pad_to = lambda v, n: jnp.pad(v, (0, n - v.shape[0]))
    qc = _mla_q(pc, _row(p["c_qa_norm"][l]), wq_pad, _row(pad_to(qg, C_HPAD)), cos64, sin_lo, sin_hi, seq)
    kc, c_vt = _mla_kv(pc, _row(p["c_kva_norm"][l]), p["c_wkv_up"][l].astype(BF16), _row(kg[:C_NOPE]),
                       _row(pad_to(kg[C_NOPE:], LANES)), cos64, sin_lo, sin_hi, seq, ATT_TK)
    oc = _mla_attn(qc, kc, c_vt, nb, seq)

    m = _merge(u, oa, ob, oc, _cast_shifted(w_in, l, AB_COLS + C_COLS, N_BRANCH * D), _cast(p["w_br_a"], l),
               _cast(p["w_br_b"], l), _cast(p["w_br_c"], l))
    x = _mm_res(m, _cast(p["w_out"], l), x, 1.0, tm=1024, tn=512, tk=D)

    return ffn(x, p["ffn2_norm"], p["ffn2_wg"], p["ffn2_wu"], p["ffn2_wd"])


def kernel(x_prompt, x_sample, ffn1_norm, ffn1_wg, ffn1_wu, ffn1_wd, mix_norm, w_in, a_qnorm, a_knorm, a_lq1, a_lk1, a_lq2, a_lk2, a_subln, b_qnorm, b_knorm, c_qa_norm, c_kva_norm, c_wq_up, c_wkv_up, c_qnorm, c_knorm, w_br_a, w_br_b, w_br_c, w_out, ffn2_norm, ffn2_wg, ffn2_wu, ffn2_wd):
    p = dict(ffn1_norm=ffn1_norm, ffn1_wg=ffn1_wg, ffn1_wu=ffn1_wu, ffn1_wd=ffn1_wd, mix_norm=mix_norm, w_in=w_in,
             a_qnorm=a_qnorm, a_knorm=a_knorm, a_lq1=a_lq1, a_lk1=a_lk1, a_lq2=a_lq2, a_lk2=a_lk2, a_subln=a_subln,
             b_qnorm=b_qnorm, b_knorm=b_knorm, c_qa_norm=c_qa_norm, c_kva_norm=c_kva_norm, c_wq_up=c_wq_up,
             c_wkv_up=c_wkv_up, c_qnorm=c_qnorm, c_knorm=c_knorm, w_br_a=w_br_a, w_br_b=w_br_b, w_br_c=w_br_c,
             w_out=w_out, ffn2_norm=ffn2_norm, ffn2_wg=ffn2_wg, ffn2_wu=ffn2_wu, ffn2_wd=ffn2_wd)
    bp, seq, D = x_prompt.shape
    bs, seq_s, _ = x_sample.shape
    assert seq == seq_s
    nb = bp + bs
    x = jnp.concatenate([x_prompt.reshape(bp * seq, D), x_sample.reshape(bs * seq, D)], axis=0)
    tabs = _rope_tables(seq)
    for l in range(ffn1_norm.shape[0]):
        x = _layer(x, l, nb, seq, tabs, p)
    return x[:bp * seq].reshape(bp, seq, D), x[bp * seq:].reshape(bs, seq, D)
```

```python
import functools
import math

import jax
import jax.numpy as jnp
from jax import lax
from jax.experimental import pallas as pl
from jax.experimental.pallas import tpu as pltpu

F32 = jnp.float32
BF16 = jnp.bfloat16

RMS_EPS = 1e-6
ROPE_THETA = 10000.0
LOG2E = math.log2(math.e)
LANES = 128
V7X_VMEM_BYTES = 64 * 1024 * 1024
VMEM_LIMIT = V7X_VMEM_BYTES - 8 * 1024 * 1024

A_HEADS, A_DH = 8, 128
A_HW = 2 * A_DH
A_W = A_HEADS * A_HW
B_GROUPS = ((128, 1), (512, 4), (2048, 16))
B_HPG, B_DH = 6, 128
B_HEADS = B_HPG * len(B_GROUPS)
B_W = B_HEADS * B_DH
B_GW = B_HPG * B_DH
C_HEADS, C_Q_RANK, C_KV_RANK, C_NOPE, C_ROPE, C_VDIM = 16, 1024, 512, 128, 64, 128
C_DQK = C_NOPE + C_ROPE
C_HPAD = 2 * LANES
C_OUT = C_HEADS * C_VDIM
N_BRANCH = 3
AB_COLS = 3 * A_W + 3 * B_W
C_COLS = C_Q_RANK + C_KV_RANK + C_ROPE
C_COLS_PAD = 1792
NEG_BIG = -1e30
ATT_TK = 512


def _tile(n, t):
    t = min(n, t)
    assert n % t == 0, (n, t)
    return t


def _params(sem):
    return pltpu.CompilerParams(dimension_semantics=sem, vmem_limit_bytes=VMEM_LIMIT)


def _row(a):
    return a.reshape(1, -1).astype(F32)


def _piece_specs(n, rows_first, tm, width, nj, two_d):
    if n == 1:
        return [pl.BlockSpec((tm, width), (lambda i, j, k: (i, j)) if two_d else (lambda i: (i, 0)))]
    na = rows_first // tm
    if two_d:
        first = lambda i, j, k: (jnp.minimum(i, na - 1), jnp.where(i < na, j, nj - 1))
        second = lambda i, j, k: (jnp.maximum(i - na, 0), jnp.where(i >= na, j, 0))
    else:
        first = lambda i: (jnp.minimum(i, na - 1), 0)
        second = lambda i: (jnp.maximum(i - na, 0), 0)
    return [pl.BlockSpec((tm, width), first), pl.BlockSpec((tm, width), second)]


def _for_piece(i, na, n, fn):
    if n == 1:
        fn(0)
    else:
        pl.when(i < na)(lambda: fn(0))
        pl.when(i >= na)(lambda: fn(1))


def _rmsnorm_kernel(*refs, na):
    x_refs, g_ref, o_ref = refs[:-2], refs[-2], refs[-1]

    def run(p):
        x = x_refs[p][...]
        y = x * lax.rsqrt(jnp.mean(x * x, axis=-1, keepdims=True) + RMS_EPS)
        o_ref[...] = (y * g_ref[...]).astype(o_ref.dtype)

    _for_piece(pl.program_id(0), na, len(x_refs), run)


def _rmsnorm(xs, g, tm=512):
    T, D = sum(x.shape[0] for x in xs), xs[0].shape[1]
    tm = _tile(math.gcd(*[x.shape[0] for x in xs]), tm)
    return pl.pallas_call(
        functools.partial(_rmsnorm_kernel, na=xs[0].shape[0] // tm),
        grid=(T // tm,),
        in_specs=_piece_specs(len(xs), xs[0].shape[0], tm, D, 1, False) + [pl.BlockSpec((1, D), lambda i: (0, 0))],
        out_specs=pl.BlockSpec((tm, D), lambda i: (i, 0)),
        out_shape=jax.ShapeDtypeStruct((T, D), BF16),
        compiler_params=_params(("parallel",)),
        name="rmsnorm",
    )(*xs, _row(g))


def _cast_kernel(x_ref, o_ref, *, rows, cols):
    x = x_ref[...]
    tr, tc = x.shape
    if rows is not None:
        r = pl.program_id(0) * tr + lax.broadcasted_iota(jnp.int32, (tr, tc), 0)
        x = jnp.where(r < rows, x, 0.0)
    if cols is not None:
        c = pl.program_id(1) * tc + lax.broadcasted_iota(jnp.int32, (tr, tc), 1)
        x = jnp.where(c < cols, x, 0.0)
    o_ref[...] = x.astype(o_ref.dtype)


def _cast(w, l, col0=0, ncols=None, rows_out=None, cols_out=None, tr=2048, tc=512):
    _, R, C = w.shape
    ncols = C - col0 if ncols is None else ncols
    rows_out = R if rows_out is None else rows_out
    cols_out = ncols if cols_out is None else cols_out
    tr, tc = min(tr, rows_out), min(tc, cols_out)
    assert col0 % tc == 0
    c0 = col0 // tc
    gr, gc = pl.cdiv(rows_out, tr), pl.cdiv(cols_out, tc)
    rows = R if gr * tr > R else None
    cols = ncols if (gc * tc > ncols and cols_out > ncols) else None
    return pl.pallas_call(
        functools.partial(_cast_kernel, rows=rows, cols=cols),
        grid=(gr, gc),
        in_specs=[pl.BlockSpec((None, tr, tc), lambda i, j: (l, i, j + c0))],
        out_specs=pl.BlockSpec((tr, tc), lambda i, j: (i, j)),
        out_shape=jax.ShapeDtypeStruct((rows_out, cols_out), BF16),
        compiler_params=_params(("parallel", "parallel")),
        name="cast_bf16",
    )(w)


def _cast_t_kernel(x_ref, o_ref, *, rows):
    x = x_ref[0]
    if rows is not None:
        r = pl.program_id(1) * x.shape[0] + lax.broadcasted_iota(jnp.int32, x.shape, 0)
        x = jnp.where(r < rows, x, 0.0)
    o_ref[...] = x.T.astype(o_ref.dtype)


def _cast_t(wt, l, row0, nrows, cols_out=None, tk=1024, tc=512):
    _, N, K = wt.shape
    cols_out = nrows if cols_out is None else cols_out
    tk, tc = min(tk, K), min(tc, cols_out)
    assert row0 % 8 == 0 and K % tk == 0 and cols_out % tc == 0
    return pl.pallas_call(
        functools.partial(_cast_t_kernel, rows=nrows if cols_out > nrows else None),
        grid=(K // tk, cols_out // tc),
        in_specs=[pl.BlockSpec((pl.Element(1), pl.Element(tc), pl.Element(tk)),
                               lambda i, j: (l, pl.multiple_of(row0 + j * tc, 8), pl.multiple_of(i * tk, LANES)))],
        out_specs=pl.BlockSpec((tk, tc), lambda i, j: (i, j)),
        out_shape=jax.ShapeDtypeStruct((K, cols_out), BF16),
        compiler_params=_params(("parallel", "parallel")),
        name="cast_bf16_t",
    )(wt)


def _ffn_up_kernel(h_ref, wg_ref, wu_ref, o_ref):
    h = h_ref[...]
    g = jnp.dot(h, wg_ref[...], preferred_element_type=F32)
    u = jnp.dot(h, wu_ref[...], preferred_element_type=F32)
    o_ref[...] = (g * jax.nn.sigmoid(g) * u).astype(o_ref.dtype)


def _ffn_up(h, wg, wu, tm=1024, tn=512):
    T, D = h.shape
    N = wg.shape[1]
    tm, tn = _tile(T, tm), _tile(N, tn)
    return pl.pallas_call(
        _ffn_up_kernel,
        grid=(T // tm, N // tn),
        in_specs=[pl.BlockSpec((tm, D), lambda i, j: (i, 0)),
                  pl.BlockSpec((D, tn), lambda i, j: (0, j)),
                  pl.BlockSpec((D, tn), lambda i, j: (0, j))],
        out_specs=pl.BlockSpec((tm, tn), lambda i, j: (i, j)),
        out_shape=jax.ShapeDtypeStruct((T, N), BF16),
        compiler_params=_params(("parallel", "parallel")),
        name="ffn_up",
    )(h, wg, wu)


def _mm_res_kernel(*refs, scale, nk, n_x, na):
    a_ref, b_ref = refs[:2]
    x_refs, o_refs = refs[2:2 + n_x], refs[2 + n_x:]
    i, k = pl.program_id(0), pl.program_id(2)
    part = jnp.dot(a_ref[...], b_ref[...], preferred_element_type=F32)

    def accumulate(p):
        x_ref, o_ref = x_refs[min(p, n_x - 1)], o_refs[min(p, len(o_refs) - 1)]

        @pl.when(k == 0)
        def _():
            o_ref[...] = part

        @pl.when(k > 0)
        def _():
            o_ref[...] += part

        @pl.when(k == nk - 1)
        def _():
            o_ref[...] = x_ref[...] + scale * o_ref[...]

    _for_piece(i, na, max(n_x, len(o_refs)), accumulate)


def _mm_res(a, b, xs, scale, out_rows=None, tm=1024, tn=1024, tk=1024):
    T, K = a.shape
    N = b.shape[1]
    out_rows = (T,) if out_rows is None else out_rows
    tm = _tile(math.gcd(*[x.shape[0] for x in xs], *out_rows), tm)
    tn, tk = _tile(N, tn), _tile(K, tk)
    nk, nj = K // tk, N // tn
    first = xs[0].shape[0] if len(xs) > 1 else out_rows[0]
    out = pl.pallas_call(
        functools.partial(_mm_res_kernel, scale=scale, nk=nk, n_x=len(xs), na=first // tm),
        grid=(T // tm, nj, nk),
        in_specs=[pl.BlockSpec((tm, tk), lambda i, j, k: (i, k)),
                  pl.BlockSpec((tk, tn), lambda i, j, k: (k, j))] + _piece_specs(len(xs), first, tm, tn, nj, True),
        out_specs=_piece_specs(len(out_rows), first, tm, tn, nj, True),
        out_shape=[jax.ShapeDtypeStruct((r, N), F32) for r in out_rows],
        compiler_params=_params(("parallel", "parallel", "arbitrary") if len(out_rows) == 1 else ("arbitrary",) * 3),
        name="mm_residual",
    )(a, b, *xs)
    return tuple(out)


def _rope128(y, cos, sin_signed):
    return y * cos + pltpu.roll(y, LANES // 2, 1) * sin_signed


def _norm_rope128(a, g, cos, sin_signed):
    return _rope128(a * lax.rsqrt(jnp.mean(a * a, axis=-1, keepdims=True) + RMS_EPS) * g, cos, sin_signed)


def _proj_rope_kernel(u_ref, w_ref, g_ref, cos_ref, sin_ref, o_ref, *, row_chunks):
    tm = u_ref.shape[0] // row_chunks
    for r in range(row_chunks):
        rows = slice(r * tm, (r + 1) * tm)
        acc = jnp.dot(u_ref[rows, :], w_ref[...], preferred_element_type=F32)
        for c in range(acc.shape[1] // LANES):
            sl = slice(c * LANES, (c + 1) * LANES)
            o_ref[rows, sl] = _norm_rope128(acc[:, sl], g_ref[:, sl], cos_ref[rows, :],
                                            sin_ref[rows, :]).astype(o_ref.dtype)


def _proj_rope(u, w, gains, cos, sin, seq, tm=1024, tn=256):
    T, D = u.shape
    N = w.shape[1]
    tm, tn = _tile(seq, tm), _tile(N, tn)
    npos = seq // tm
    return pl.pallas_call(
        functools.partial(_proj_rope_kernel, row_chunks=4 if tm % 1024 == 0 else 1),
        grid=(T // tm, N // tn),
        in_specs=[pl.BlockSpec((tm, D), lambda i, j: (i, 0)),
                  pl.BlockSpec((D, tn), lambda i, j: (0, j)),
                  pl.BlockSpec((1, tn), lambda i, j: (0, j)),
                  pl.BlockSpec((tm, LANES), lambda i, j: (i % npos, 0)),
                  pl.BlockSpec((tm, LANES), lambda i, j: (i % npos, 0))],
        out_specs=pl.BlockSpec((tm, tn), lambda i, j: (i, j)),
        out_shape=jax.ShapeDtypeStruct((T, N), BF16),
        compiler_params=_params(("parallel", "parallel")),
        name="proj_rope",
    )(u, w, gains, cos, sin)


def _proj_vt_kernel(u_ref, w_ref, o_ref):
    acc = jnp.dot(u_ref[...], w_ref[...], preferred_element_type=F32)
    tk = o_ref.shape[3]
    for c in range(o_ref.shape[1]):
        o_ref[0, c] = acc[c * tk:(c + 1) * tk, :].T.astype(o_ref.dtype)


def _proj_vt(u, w, heads, tk, tm=1024):
    T, D = u.shape
    hw = w.shape[1] // heads
    tm = _tile(T, tm)
    tk = _tile(tm, tk)
    return pl.pallas_call(
        _proj_vt_kernel,
        grid=(T // tm, heads),
        in_specs=[pl.BlockSpec((tm, D), lambda i, h: (i, 0)), pl.BlockSpec((D, hw), lambda i, h: (0, h))],
        out_specs=pl.BlockSpec((1, tm // tk, hw, tk), lambda i, h: (h, i, 0, 0)),
        out_shape=jax.ShapeDtypeStruct((heads, T // tk, hw, tk), BF16),
        compiler_params=_params(("parallel", "parallel")),
        name="proj_vt",
    )(u, w)


def _proj_b_kernel(u_ref, w_ref, g_ref, cos_ref, sin_ref, o_ref, scr, *, dil):
    t = pl.program_id(1)
    heads = [slice(h * B_DH, (h + 1) * B_DH) for h in range(B_HPG)]
    tm = u_ref.shape[0]
    row_chunks = 4 if tm % 1024 == 0 else 1
    chunks = [slice(r * (tm // row_chunks), (r + 1) * (tm // row_chunks)) for r in range(row_chunks)]

    @pl.when(t < 2)
    def _():
        for rows in chunks:
            acc = jnp.dot(u_ref[rows, :], w_ref[...], preferred_element_type=F32)
            for h, sl in enumerate(heads):
                scr[h, rows, :] = _norm_rope128(acc[:, sl], g_ref[:, sl], cos_ref[rows, :], sin_ref[rows, :])

    @pl.when(t == 2)
    def _():
        for rows in chunks:
            acc = jnp.dot(u_ref[rows, :], w_ref[...], preferred_element_type=F32)
            for h, sl in enumerate(heads):
                scr[h, rows, :] = acc[:, sl]

    n = tm // dil
    for c in range(dil):
        for h, sl in enumerate(heads):
            o_ref[c, :, sl] = scr[h, pl.ds(c, n, stride=dil), :].astype(o_ref.dtype)


def _proj_b(u, w_b, gains, cos, sin, g, nb, seq, tm=1024):
    T, D = u.shape
    dil = B_GROUPS[g][1]
    tm = _tile(seq, tm)
    assert tm % dil == 0 and (tm // dil) % 16 == 0
    npos = seq // tm
    ng = len(B_GROUPS)
    return pl.pallas_call(
        functools.partial(_proj_b_kernel, dil=dil),
        grid=(T // tm, 3),
        in_specs=[pl.BlockSpec((tm, D), lambda i, t: (i, 0)),
                  pl.BlockSpec((D, B_GW), lambda i, t: (0, t * ng + g)),
                  pl.BlockSpec((1, B_GW), lambda i, t: (0, t)),
                  pl.BlockSpec((tm, LANES), lambda i, t: (i % npos, 0)),
                  pl.BlockSpec((tm, LANES), lambda i, t: (i % npos, 0))],
        out_specs=pl.BlockSpec((dil, tm // dil, B_GW), lambda i, t: (i // npos, i % npos, t)),
        out_shape=jax.ShapeDtypeStruct((nb * dil, seq // dil, 3 * B_GW), BF16),
        scratch_shapes=[pltpu.VMEM((B_HPG, tm, B_DH), F32)],
        compiler_params=_params(("parallel", "arbitrary")),
        name=f"proj_b_d{dil}",
    )(u, w_b, gains, cos, sin)


def _mm_kernel(a_ref, b_ref, o_ref):
    o_ref[...] = jnp.dot(a_ref[...], b_ref[...], preferred_element_type=F32).astype(o_ref.dtype)


def _mm(a, b, out_dtype, tm=1024, tn=256):
    T, K = a.shape
    N = b.shape[1]
    tm, tn = _tile(T, tm), _tile(N, tn)
    return pl.pallas_call(
        _mm_kernel,
        grid=(T // tm, N // tn),
        in_specs=[pl.BlockSpec((tm, K), lambda i, j: (i, 0)), pl.BlockSpec((K, tn), lambda i, j: (0, j))],
        out_specs=pl.BlockSpec((tm, tn), lambda i, j: (i, j)),
        out_shape=jax.ShapeDtypeStruct((T, N), out_dtype),
        compiler_params=_params(("parallel", "parallel")),
        name="mm",
    )(a, b)


def _rope64_in128(y, cos, sin_lo, sin_hi):
    q = C_ROPE // 2
    return y * cos + pltpu.roll(y, LANES - q, 1) * sin_lo + pltpu.roll(y, q, 1) * sin_hi


def _mla_q_kernel(cq_ref, ng_ref, w_ref, g_ref, cos_ref, slo_ref, shi_ref, o_ref, a_scr, *, tn):
    @pl.when(pl.program_id(1) == 0)
    def _():
        x = cq_ref[...]
        y = x * lax.rsqrt(jnp.mean(x * x, axis=-1, keepdims=True) + RMS_EPS) * ng_ref[...]
        a_scr[...] = y.astype(a_scr.dtype)

    tm = a_scr.shape[0]
    row_chunks = 2 if tm % 512 == 0 else 1
    for rc in range(row_chunks):
        rows = slice(rc * (tm // row_chunks), (rc + 1) * (tm // row_chunks))
        acc = jnp.dot(a_scr[rows, :], w_ref[...], preferred_element_type=F32)
        for h in range(tn // C_HPAD):
            a = acc[:, h * C_HPAD:(h + 1) * C_HPAD]
            r = lax.rsqrt(jnp.sum(a * a, axis=-1, keepdims=True) * (1.0 / C_DQK) + RMS_EPS)
            y = a * r * g_ref[...]
            o_ref[rows, h * C_HPAD:h * C_HPAD + LANES] = y[:, :LANES].astype(o_ref.dtype)
            o_ref[rows, h * C_HPAD + LANES:(h + 1) * C_HPAD] = _rope64_in128(
                y[:, LANES:], cos_ref[rows, :], slo_ref[rows, :], shi_ref[rows, :]).astype(o_ref.dtype)


def _mla_q(pc, qa_gain, wq_pad, q_gain_pad, cos, slo, shi, seq, tm=512, tn=512):
    T = pc.shape[0]
    N = wq_pad.shape[1]
    tm, tn = _tile(seq, tm), _tile(N, tn)
    npos = seq // tm
    pos = lambda i, j: (i % npos, 0)
    return pl.pallas_call(
        functools.partial(_mla_q_kernel, tn=tn),
        grid=(T // tm, N // tn),
        in_specs=[pl.BlockSpec((tm, C_Q_RANK), lambda i, j: (i, 0)),
                  pl.BlockSpec((1, C_Q_RANK), lambda i, j: (0, 0)),
                  pl.BlockSpec((C_Q_RANK, tn), lambda i, j: (0, j)),
                  pl.BlockSpec((1, C_HPAD), lambda i, j: (0, 0)),
                  pl.BlockSpec((tm, LANES), pos), pl.BlockSpec((tm, LANES), pos), pl.BlockSpec((tm, LANES), pos)],
        out_specs=pl.BlockSpec((tm, tn), lambda i, j: (i, j)),
        out_shape=jax.ShapeDtypeStruct((T, N), BF16),
        scratch_shapes=[pltpu.VMEM((tm, C_Q_RANK), BF16)],
        compiler_params=_params(("parallel", "arbitrary")),
        name="mla_q_up",
    )(pc, qa_gain, wq_pad, q_gain_pad, cos, slo, shi)


def _mla_kv_kernel(ckv_ref, ng_ref, kr_ref, w_ref, gn_ref, gr_ref, cos_ref, slo_ref, shi_ref,
                   k_ref, vt_ref, a_scr, *, heads):
    @pl.when(pl.program_id(1) == 0)
    def _():
        x = ckv_ref[...]
        y = x * lax.rsqrt(jnp.mean(x * x, axis=-1, keepdims=True) + RMS_EPS) * ng_ref[...]
        a_scr[...] = y.astype(a_scr.dtype)

    acc = jnp.dot(a_scr[...], w_ref[...], preferred_element_type=F32)
    kr = kr_ref[...]
    kr_ss = jnp.sum(kr * kr, axis=-1, keepdims=True)
    tk = vt_ref.shape[3]
    for h in range(heads):
        base = h * (C_NOPE + C_VDIM)
        kn = acc[:, base:base + C_NOPE]
        r = lax.rsqrt((jnp.sum(kn * kn, axis=-1, keepdims=True) + kr_ss) * (1.0 / C_DQK) + RMS_EPS)
        k_ref[:, h * C_HPAD:h * C_HPAD + LANES] = (kn * r * gn_ref[...]).astype(k_ref.dtype)
        k_ref[:, h * C_HPAD + LANES:(h + 1) * C_HPAD] = _rope64_in128(
            kr * r * gr_ref[...], cos_ref[...], slo_ref[...], shi_ref[...]).astype(k_ref.dtype)
        vt = acc[:, base + C_NOPE:base + C_NOPE + C_VDIM].T
        for c in range(vt_ref.shape[1]):
            vt_ref[h, c] = vt[:, c * tk:(c + 1) * tk].astype(vt_ref.dtype)


def _mla_kv(pc, kva_gain, wkv, k_gain_nope, k_gain_rope, cos, slo, shi, seq, tk, tm=512, heads=2):
    T = pc.shape[0]
    tm = _tile(seq, tm)
    tk = _tile(tm, tk)
    npos = seq // tm
    pos = lambda i, j: (i % npos, 0)
    hw = C_NOPE + C_VDIM
    return pl.pallas_call(
        functools.partial(_mla_kv_kernel, heads=heads),
        grid=(T // tm, C_HEADS // heads),
        in_specs=[pl.BlockSpec((tm, C_KV_RANK), lambda i, j: (i, C_Q_RANK // C_KV_RANK)),
                  pl.BlockSpec((1, C_KV_RANK), lambda i, j: (0, 0)),
                  pl.BlockSpec((tm, LANES), lambda i, j: (i, (C_Q_RANK + C_KV_RANK) // LANES)),
                  pl.BlockSpec((C_KV_RANK, heads * hw), lambda i, j: (0, j)),
                  pl.BlockSpec((1, LANES), lambda i, j: (0, 0)),
                  pl.BlockSpec((1, LANES), lambda i, j: (0, 0)),
                  pl.BlockSpec((tm, LANES), pos), pl.BlockSpec((tm, LANES), pos), pl.BlockSpec((tm, LANES), pos)],
        out_specs=[pl.BlockSpec((tm, heads * C_HPAD), lambda i, j: (i, j)),
                   pl.BlockSpec((heads, tm // tk, C_VDIM, tk), lambda i, j: (j, i, 0, 0))],
        out_shape=[jax.ShapeDtypeStruct((T, C_HEADS * C_HPAD), BF16),
                   jax.ShapeDtypeStruct((C_HEADS, T // tk, C_VDIM, tk), BF16)],
        scratch_shapes=[pltpu.VMEM((tm, C_KV_RANK), BF16)],
        compiler_params=_params(("parallel", "arbitrary")),
        name="mla_kv_up",
    )(pc, kva_gain, pc, wkv, k_gain_nope, k_gain_rope, cos, slo, shi)


def _nt_dot(a, b):
    return lax.dot_general(a, b, (((1,), (1,)), ((), ())), preferred_element_type=F32)


def _scores_t(k, q, s_ref):
    s = _nt_dot(k, q)
    s_ref[...] = s
    return jnp.max(s, axis=0, keepdims=True)


def _softmax_step_t(s_ref, s_max, vt, m_prev, l_prev, acc_ref, idx):
    m_new = jnp.maximum(m_prev, s_max)
    alpha = jnp.exp2(m_prev - m_new)
    p = jnp.exp2(s_ref[...] - m_new)
    l_new = alpha * l_prev + jnp.sum(p, axis=0, keepdims=True)
    acc_ref[idx] = acc_ref[idx] * alpha + jnp.dot(vt, p.astype(vt.dtype), preferred_element_type=F32)
    return m_new, l_new


def _pipelined_key_loop(nk, scores, consume, init):
    assert nk % 2 == 0

    def pair(jj, carry):
        state, smax0 = carry
        j = 2 * jj
        smax1 = scores(1, j + 1)
        state = consume(0, j, smax0, state)
        smax0 = scores(0, j + 2)
        state = consume(1, j + 1, smax1, state)
        return state, smax0

    state, smax0 = lax.fori_loop(0, nk // 2 - 1, pair, (init, scores(0, 0)))
    smax1 = scores(1, nk - 1)
    state = consume(0, nk - 2, smax0, state)
    return consume(1, nk - 1, smax1, state)


def _diff_attn_kernel(q_ref, k_ref, vt_ref, lq1_ref, lk1_ref, lq2_ref, lk2_ref, g_ref, o_ref, acc_ref, s_ref,
                      *, lam_init):
    tq = q_ref.shape[0]
    tk = vt_ref.shape[3]
    nk = k_ref.shape[0] // tk
    q = q_ref[...]
    acc_ref[...] = jnp.zeros_like(acc_ref)
    subs = [slice(sub * A_DH, (sub + 1) * A_DH) for sub in range(2)]

    def scores(slot, j):
        k = k_ref[pl.ds(pl.multiple_of(j * tk, tk), tk), :]
        return tuple(_scores_t(k[:, sl], q[:, sl], s_ref.at[slot, sub]) for sub, sl in enumerate(subs))

    def consume(slot, j, smax, state):
        vt = vt_ref[0, j]
        m0, l0, m1, l1 = state
        m0, l0 = _softmax_step_t(s_ref.at[slot, 0], smax[0], vt, m0, l0, acc_ref, 0)
        m1, l1 = _softmax_step_t(s_ref.at[slot, 1], smax[1], vt, m1, l1, acc_ref, 1)
        return m0, l0, m1, l1

    neg = jnp.full((1, tq), -jnp.inf, F32)
    zero = jnp.zeros((1, tq), F32)
    _, l0, _, l1 = _pipelined_key_loop(nk, scores, consume, (neg, zero, neg, zero))

    lam = (jnp.exp(jnp.sum(lq1_ref[...] * lk1_ref[...], axis=-1, keepdims=True))
           - jnp.exp(jnp.sum(lq2_ref[...] * lk2_ref[...], axis=-1, keepdims=True)) + lam_init)
    o = (acc_ref[0] * (1.0 / l0) - lam * (acc_ref[1] * (1.0 / l1))).T
    y = o * lax.rsqrt(jnp.mean(o * o, axis=-1, keepdims=True) + RMS_EPS) * g_ref[...]
    o_ref[...] = (y * (1.0 - lam_init)).astype(o_ref.dtype)


def _diff_attn(qk, vt, lq1, lk1, lq2, lk2, subln, nb, seq, lam_init, tq=1024):
    T = qk.shape[0]
    tk = vt.shape[3]
    tq = _tile(seq, tq)
    nq = seq // tq
    small = pl.BlockSpec((1, A_DH), lambda b, h, i: (0, 0))
    return pl.pallas_call(
        functools.partial(_diff_attn_kernel, lam_init=lam_init),
        grid=(nb, A_HEADS, nq),
        in_specs=[pl.BlockSpec((tq, A_HW), lambda b, h, i: (b * nq + i, h)),
                  pl.BlockSpec((seq, A_HW), lambda b, h, i: (b, A_HEADS + h)),
                  pl.BlockSpec((1, seq // tk, A_HW, tk), lambda b, h, i: (h, b, 0, 0)),
                  small, small, small, small,
                  pl.BlockSpec((1, A_HW), lambda b, h, i: (0, 0))],
        out_specs=pl.BlockSpec((tq, A_HW), lambda b, h, i: (b * nq + i, h)),
        out_shape=jax.ShapeDtypeStruct((T, A_W), BF16),
        scratch_shapes=[pltpu.VMEM((2, A_HW, tq), F32), pltpu.VMEM((2, 2, tk, tq), F32)],
        compiler_params=_params(("parallel", "parallel", "parallel")),
        name="diff_attn",
    )(qk, qk, vt, _row(lq1), _row(lk1), _row(lq2), _row(lk2), _row(subln))


def _mla_attn_kernel(q_ref, k_ref, vt_ref, o_ref, acc_ref, s_ref):
    tq = q_ref.shape[0]
    tk = vt_ref.shape[3]
    nk = k_ref.shape[0] // tk
    q = q_ref[...]
    acc_ref[...] = jnp.zeros_like(acc_ref)

    def scores(slot, j):
        return _scores_t(k_ref[pl.ds(pl.multiple_of(j * tk, tk), tk), :], q, s_ref.at[slot])

    def consume(slot, j, smax, state):
        return _softmax_step_t(s_ref.at[slot], smax, vt_ref[0, j], state[0], state[1], acc_ref, 0)

    init = (jnp.full((1, tq), -jnp.inf, F32), jnp.zeros((1, tq), F32))
    _, l = _pipelined_key_loop(nk, scores, consume, init)
    o_ref[...] = (acc_ref[0] * (1.0 / l)).T.astype(o_ref.dtype)


def _mla_attn(qc, kc, vt, nb, seq, tq=1024):
    T = qc.shape[0]
    tk = vt.shape[3]
    tq = _tile(seq, tq)
    nq = seq // tq
    return pl.pallas_call(
        _mla_attn_kernel,
        grid=(nb, C_HEADS, nq),
        in_specs=[pl.BlockSpec((tq, C_HPAD), lambda b, h, i: (b * nq + i, h)),
                  pl.BlockSpec((seq, C_HPAD), lambda b, h, i: (b, h)),
                  pl.BlockSpec((1, seq // tk, C_VDIM, tk), lambda b, h, i: (h, b, 0, 0))],
        out_specs=pl.BlockSpec((tq, C_VDIM), lambda b, h, i: (b * nq + i, h)),
        out_shape=jax.ShapeDtypeStruct((T, C_OUT), BF16),
        scratch_shapes=[pltpu.VMEM((1, C_VDIM, tq), F32), pltpu.VMEM((2, tk, tq), F32)],
        compiler_params=_params(("parallel", "parallel", "parallel")),
        name="mla_attn",
    )(qc, kc, vt)


def _window_attn_kernel(q_ref, kp_ref, km_ref, kn_ref, vp_ref, vm_ref, vn_ref, o_ref, lse_ref, *, half, sub_len):
    tq = q_ref.shape[0]
    i = pl.program_id(1)
    span = tq + 2 * half
    qpos = i * tq + lax.broadcasted_iota(jnp.int32, (tq, span), 0)
    kpos = i * tq - half + lax.broadcasted_iota(jnp.int32, (tq, span), 1)
    valid = (jnp.abs(kpos - qpos) <= half) & (kpos >= 0) & (kpos < sub_len)
    scale = B_DH ** -0.5
    for h in range(B_HPG):
        sl = slice(h * B_DH, (h + 1) * B_DH)
        k = jnp.concatenate([kp_ref[:, sl], km_ref[:, sl], kn_ref[:, sl]], axis=0)
        v = jnp.concatenate([vp_ref[:, sl], vm_ref[:, sl], vn_ref[:, sl]], axis=0)
        s = jnp.where(valid, _nt_dot(q_ref[:, sl], k) * scale, NEG_BIG)
        m = jnp.max(s, axis=-1, keepdims=True)
        p = jnp.exp(s - m)
        l = jnp.sum(p, axis=-1, keepdims=True)
        o = jnp.dot(p.astype(v.dtype), v, preferred_element_type=F32) / l
        o_ref[:, sl] = o.astype(o_ref.dtype)
        lse_ref[:, sl] = jnp.broadcast_to(m + jnp.log(l), (tq, B_DH))


def _window_attn(qkv, g, tq=256):
    win, dil = B_GROUPS[g]
    half = win // (2 * dil)
    nsub, L, _ = qkv.shape
    tq = _tile(L, tq)
    assert tq % half == 0
    nq = L // tq
    r = tq // half
    nh = L // half
    flat = qkv.reshape(nsub * L, 3 * B_GW)

    def main(col):
        return pl.BlockSpec((tq, B_GW), lambda s, i: (s * nq + i, col))

    def prev(col):
        return pl.BlockSpec((half, B_GW), lambda s, i: (s * nh + jnp.maximum(i * r - 1, 0), col))

    def nxt(col):
        return pl.BlockSpec((half, B_GW), lambda s, i: (s * nh + jnp.minimum((i + 1) * r, nh - 1), col))

    out_spec = pl.BlockSpec((tq, B_GW), lambda s, i: (s * nq + i, 0))
    o, lse = pl.pallas_call(
        functools.partial(_window_attn_kernel, half=half, sub_len=L),
        grid=(nsub, nq),
        in_specs=[main(0), prev(1), main(1), nxt(1), prev(2), main(2), nxt(2)],
        out_specs=[out_spec, out_spec],
        out_shape=[jax.ShapeDtypeStruct((nsub * L, B_GW), BF16), jax.ShapeDtypeStruct((nsub * L, B_GW), F32)],
        compiler_params=_params(("parallel", "parallel")),
        name=f"window_attn_d{dil}",
    )(flat, flat, flat, flat, flat, flat, flat)
    return o.reshape(nsub, L, B_GW), lse.reshape(nsub, L, B_GW)


def _combine_kernel(o0, o1, o2, l0, l1, l2, o_ref, so, sl):
    tm = o_ref.shape[0]
    heads = [slice(h * B_DH, (h + 1) * B_DH) for h in range(B_HPG)]
    for g, (og, lg) in enumerate(((o0, l0), (o1, l1), (o2, l2))):
        dil = og.shape[0]
        n = tm // dil
        for c in range(dil):
            for h, hs in enumerate(heads):
                so[g * B_HPG + h, pl.ds(c, n, stride=dil), :] = og[c, :, hs].astype(F32)
                sl[g * B_HPG + h, pl.ds(c, n, stride=dil), :] = lg[c, :, hs]
    for h, hs in enumerate(heads):
        a0, a1, a2 = sl[h], sl[B_HPG + h], sl[2 * B_HPG + h]
        m = jnp.maximum(jnp.maximum(a0, a1), a2)
        e0, e1, e2 = jnp.exp(a0 - m), jnp.exp(a1 - m), jnp.exp(a2 - m)
        num = e0 * so[h] + e1 * so[B_HPG + h] + e2 * so[2 * B_HPG + h]
        o_ref[:, hs] = (num / (e0 + e1 + e2)).astype(o_ref.dtype)


def _combine(outs, lses, nb, seq, tm=512):
    tm = _tile(seq, tm)
    npos = seq // tm
    ng = len(outs)

    def spec(a):
        dil = a.shape[0] // nb
        return pl.BlockSpec((dil, tm // dil, B_GW), lambda i: (i // npos, i % npos, 0))

    return pl.pallas_call(
        _combine_kernel,
        grid=(nb * npos,),
        in_specs=[spec(a) for a in outs] + [spec(a) for a in lses],
        out_specs=pl.BlockSpec((tm, B_GW), lambda i: (i, 0)),
        out_shape=jax.ShapeDtypeStruct((nb * seq, B_GW), BF16),
        scratch_shapes=[pltpu.VMEM((ng * B_HPG, tm, B_DH), F32), pltpu.VMEM((ng * B_HPG, tm, B_DH), F32)],
        compiler_params=_params(("parallel",)),
        name="window_combine",
    )(*outs, *lses)


def _merge_kernel(u_ref, oa_ref, ob_ref, oc_ref, wga_ref, wgb_ref, wgc_ref, wa_ref, wb_ref, wc_ref, o_ref):
    u = u_ref[...]

    def branch(wg_ref, x_ref, w_ref):
        gate = jax.nn.sigmoid(jnp.dot(u, wg_ref[...], preferred_element_type=F32))
        return gate * jnp.dot(x_ref[...], w_ref[...], preferred_element_type=F32)

    o_ref[...] = (branch(wga_ref, oa_ref, wa_ref) + branch(wgb_ref, ob_ref, wb_ref)
                  + branch(wgc_ref, oc_ref, wc_ref)).astype(o_ref.dtype)


def _merge(u, oa, ob, oc, w_gates, wa, wb, wc, tm=512, tn=256):
    T, D = u.shape
    tm, tn = _tile(T, tm), _tile(D, tn)
    nj = D // tn
    row = lambda w: pl.BlockSpec((tm, w), lambda i, j: (i, 0))
    col = lambda k: pl.BlockSpec((k, tn), lambda i, j: (0, j))
    gate = lambda n: pl.BlockSpec((D, tn), lambda i, j: (0, n * nj + j))
    return pl.pallas_call(
        _merge_kernel,
        grid=(T // tm, nj),
        in_specs=[row(D), row(oa.shape[1]), row(ob.shape[1]), row(oc.shape[1]),
                  gate(0), gate(1), gate(2), col(wa.shape[0]), col(wb.shape[0]), col(wc.shape[0])],
        out_specs=pl.BlockSpec((tm, tn), lambda i, j: (i, j)),
        out_shape=jax.ShapeDtypeStruct((T, D), BF16),
        compiler_params=_params(("parallel", "parallel")),
        name="gated_merge",
    )(u, oa, ob, oc, w_gates, w_gates, w_gates, wa, wb, wc)


def _rope_tables(seq):
    pos = jnp.arange(seq, dtype=F32)[:, None]

    def ang(d):
        half = d // 2
        inv = 1.0 / (ROPE_THETA ** (jnp.arange(half, dtype=F32) * (2.0 / d)))
        return pos * inv[None, :]

    a = ang(A_DH)
    cos128 = jnp.concatenate([jnp.cos(a), jnp.cos(a)], axis=-1)
    sin128 = jnp.concatenate([-jnp.sin(a), jnp.sin(a)], axis=-1)
    c = ang(C_ROPE)
    z = jnp.zeros_like(c)
    cos64 = jnp.concatenate([jnp.cos(c), jnp.cos(c), z, z], axis=-1)
    sin_lo = jnp.concatenate([-jnp.sin(c), z, z, z], axis=-1)
    sin_hi = jnp.concatenate([z, jnp.sin(c), z, z], axis=-1)
    return cos128, sin128, cos64, sin_lo, sin_hi


def _layer(x, l, nb, seq, tabs, p, out_rows):
    cos128, sin128, cos64, sin_lo, sin_hi = tabs
    D = x[0].shape[1]
    ff = p["ffn1_wg"].shape[2]
    ffp = -(-ff // 1024) * 1024

    def ffn(x, norm, wg, wu, wd, out_rows=None):
        h = _rmsnorm(x, norm[l])
        act = _ffn_up(h, _cast(wg, l, cols_out=ffp), _cast(wu, l, cols_out=ffp))
        return _mm_res(act, _cast(wd, l, rows_out=ffp), x, 0.5, out_rows, tk=ffp // 4)

    x = ffn(x, p["ffn1_norm"], p["ffn1_wg"], p["ffn1_wu"], p["ffn1_wd"])

    u = _rmsnorm(x, p["mix_norm"][l])
    w_in_t = jnp.swapaxes(p["w_in"], 1, 2)
    b0 = 3 * A_W
    tile_h = lambda g, n: jnp.tile(g, n)

    a_gains = jnp.concatenate([tile_h(p["a_qnorm"][l] * (A_DH ** -0.5 * LOG2E), A_W // A_DH),
                               tile_h(p["a_knorm"][l], A_W // A_DH)]).reshape(1, 2 * A_W)
    a_qk = _proj_rope(u, _cast_t(w_in_t, l, 0, 2 * A_W), a_gains, cos128, sin128, seq)
    a_vt = _proj_vt(u, _cast_t(w_in_t, l, 2 * A_W, A_W), A_HEADS, ATT_TK)
    lam_init = 0.8 - 0.6 * math.exp(-0.3 * l)
    oa = _diff_attn(a_qk, a_vt, p["a_lq1"][l], p["a_lk1"][l], p["a_lq2"][l], p["a_lk2"][l], p["a_subln"][l],
                    nb, seq, lam_init)

    b_gains = jnp.concatenate([tile_h(p["b_qnorm"][l], B_HPG), tile_h(p["b_knorm"][l], B_HPG),
                               jnp.ones((B_GW,), F32)]).reshape(1, 3 * B_GW)
    w_b = _cast_t(w_in_t, l, b0, 3 * B_W, tc=256)
    wres = [_window_attn(_proj_b(u, w_b, b_gains, cos128, sin128, g, nb, seq), g) for g in range(len(B_GROUPS))]
    ob = _combine([r[0] for r in wres], [r[1] for r in wres], nb, seq)

    pc = _mm(u, _cast_t(w_in_t, l, AB_COLS, C_COLS, cols_out=C_COLS_PAD, tc=256), F32)
    wq = p["c_wq_up"][l].reshape(C_Q_RANK, C_HEADS, C_DQK)
    wq_pad = jnp.pad(wq, ((0, 0), (0, 0), (0, C_HPAD - C_DQK))).reshape(C_Q_RANK, C_HEADS * C_HPAD).astype(BF16)
    qg = p["c_qnorm"][l] * (C_DQK ** -0.5 * LOG2E)
    kg = p["c_knorm"][l]
    pad_to = lambda v, n: jnp.pad(v, (0, n - v.shape[0]))
    qc = _mla_q(pc, _row(p["c_qa_norm"][l]), wq_pad, _row(pad_to(qg, C_HPAD)), cos64, sin_lo, sin_hi, seq)
    kc, c_vt = _mla_kv(pc, _row(p["c_kva_norm"][l]), p["c_wkv_up"][l].astype(BF16), _row(kg[:C_NOPE]),
                       _row(pad_to(kg[C_NOPE:], LANES)), cos64, sin_lo, sin_hi, seq, ATT_TK)
    oc = _mla_attn(qc, kc, c_vt, nb, seq)

    m = _merge(u, oa, ob, oc, _cast_t(w_in_t, l, AB_COLS + C_COLS, N_BRANCH * D), _cast(p["w_br_a"], l),
               _cast(p["w_br_b"], l), _cast(p["w_br_c"], l))
    x = _mm_res(m, _cast(p["w_out"], l), x, 1.0, tm=1024, tn=512, tk=D)

    return ffn(x, p["ffn2_norm"], p["ffn2_wg"], p["ffn2_wu"], p["ffn2_wd"], out_rows)


def kernel(x_prompt, x_sample, ffn1_norm, ffn1_wg, ffn1_wu, ffn1_wd, mix_norm, w_in, a_qnorm, a_knorm, a_lq1, a_lk1, a_lq2, a_lk2, a_subln, b_qnorm, b_knorm, c_qa_norm, c_kva_norm, c_wq_up, c_wkv_up, c_qnorm, c_knorm, w_br_a, w_br_b, w_br_c, w_out, ffn2_norm, ffn2_wg, ffn2_wu, ffn2_wd):
    p = dict(ffn1_norm=ffn1_norm, ffn1_wg=ffn1_wg, ffn1_wu=ffn1_wu, ffn1_wd=ffn1_wd, mix_norm=mix_norm, w_in=w_in,
             a_qnorm=a_qnorm, a_knorm=a_knorm, a_lq1=a_lq1, a_lk1=a_lk1, a_lq2=a_lq2, a_lk2=a_lk2, a_subln=a_subln,
             b_qnorm=b_qnorm, b_knorm=b_knorm, c_qa_norm=c_qa_norm, c_kva_norm=c_kva_norm, c_wq_up=c_wq_up,
             c_wkv_up=c_wkv_up, c_qnorm=c_qnorm, c_knorm=c_knorm, w_br_a=w_br_a, w_br_b=w_br_b, w_br_c=w_br_c,
             w_out=w_out, ffn2_norm=ffn2_norm, ffn2_wg=ffn2_wg, ffn2_wu=ffn2_wu, ffn2_wd=ffn2_wd)
    bp, seq, D = x_prompt.shape
    bs, seq_s, _ = x_sample.shape
    assert seq == seq_s
    nb = bp + bs
    x = (x_prompt.reshape(bp * seq, D), x_sample.reshape(bs * seq, D))
    tabs = _rope_tables(seq)
    depth = ffn1_norm.shape[0]
    for l in range(depth):
        x = _layer(x, l, nb, seq, tabs, p, (bp * seq, bs * seq) if l == depth - 1 else None)
    return x[0].reshape(bp, seq, D), x[1].reshape(bs, seq, D)
```

```python
import functools
import math

import jax
import jax.numpy as jnp
from jax import lax
from jax.experimental import pallas as pl
from jax.experimental.pallas import tpu as pltpu

F32 = jnp.float32
BF16 = jnp.bfloat16

RMS_EPS = 1e-6
ROPE_THETA = 10000.0
LOG2E = math.log2(math.e)
LANES = 128
V7X_VMEM_BYTES = 64 * 1024 * 1024
VMEM_LIMIT = V7X_VMEM_BYTES - 8 * 1024 * 1024

A_HEADS, A_DH = 8, 128
A_HW = 2 * A_DH
A_W = A_HEADS * A_HW
B_GROUPS = ((128, 1), (512, 4), (2048, 16))
B_HPG, B_DH = 6, 128
B_HEADS = B_HPG * len(B_GROUPS)
B_W = B_HEADS * B_DH
B_GW = B_HPG * B_DH
C_HEADS, C_Q_RANK, C_KV_RANK, C_NOPE, C_ROPE, C_VDIM = 16, 1024, 512, 128, 64, 128
C_DQK = C_NOPE + C_ROPE
C_HPAD = 2 * LANES
C_OUT = C_HEADS * C_VDIM
C_VT_ROWS = C_VDIM + 16
N_BRANCH = 3
AB_COLS = 3 * A_W + 3 * B_W
C_COLS = C_Q_RANK + C_KV_RANK + C_ROPE
C_COLS_PAD = 1792
NEG_BIG = -1e30
ATT_TK = 1024


def _tile(n, t):
    t = min(n, t)
    assert n % t == 0, (n, t)
    return t


def _params(sem):
    return pltpu.CompilerParams(dimension_semantics=sem, vmem_limit_bytes=VMEM_LIMIT)


def _row(a):
    return a.reshape(1, -1).astype(F32)


def _piece_specs(n, rows_first, tm, width, nj, two_d):
    if n == 1:
        return [pl.BlockSpec((tm, width), (lambda i, j, k: (i, j)) if two_d else (lambda i: (i, 0)))]
    na = rows_first // tm
    if two_d:
        first = lambda i, j, k: (jnp.minimum(i, na - 1), jnp.where(i < na, j, nj - 1))
        second = lambda i, j, k: (jnp.maximum(i - na, 0), jnp.where(i >= na, j, 0))
    else:
        first = lambda i: (jnp.minimum(i, na - 1), 0)
        second = lambda i: (jnp.maximum(i - na, 0), 0)
    return [pl.BlockSpec((tm, width), first), pl.BlockSpec((tm, width), second)]


def _for_piece(i, na, n, fn):
    if n == 1:
        fn(0)
    else:
        pl.when(i < na)(lambda: fn(0))
        pl.when(i >= na)(lambda: fn(1))


def _rmsnorm_kernel(*refs, na):
    x_refs, g_ref, o_ref = refs[:-2], refs[-2], refs[-1]

    def run(p):
        x = x_refs[p][...]
        y = x * lax.rsqrt(jnp.mean(x * x, axis=-1, keepdims=True) + RMS_EPS)
        o_ref[...] = (y * g_ref[...]).astype(o_ref.dtype)

    _for_piece(pl.program_id(0), na, len(x_refs), run)


def _rmsnorm(xs, g, tm=512):
    T, D = sum(x.shape[0] for x in xs), xs[0].shape[1]
    tm = _tile(math.gcd(*[x.shape[0] for x in xs]), tm)
    return pl.pallas_call(
        functools.partial(_rmsnorm_kernel, na=xs[0].shape[0] // tm),
        grid=(T // tm,),
        in_specs=_piece_specs(len(xs), xs[0].shape[0], tm, D, 1, False) + [pl.BlockSpec((1, D), lambda i: (0, 0))],
        out_specs=pl.BlockSpec((tm, D), lambda i: (i, 0)),
        out_shape=jax.ShapeDtypeStruct((T, D), BF16),
        compiler_params=_params(("parallel",)),
        name="rmsnorm",
    )(*xs, _row(g))


def _cast_kernel(x_ref, o_ref, *, rows, cols):
    x = x_ref[...]
    tr, tc = x.shape
    if rows is not None:
        r = pl.program_id(0) * tr + lax.broadcasted_iota(jnp.int32, (tr, tc), 0)
        x = jnp.where(r < rows, x, 0.0)
    if cols is not None:
        c = pl.program_id(1) * tc + lax.broadcasted_iota(jnp.int32, (tr, tc), 1)
        x = jnp.where(c < cols, x, 0.0)
    o_ref[...] = x.astype(o_ref.dtype)


def _cast(w, l, col0=0, ncols=None, rows_out=None, cols_out=None, tr=2048, tc=512):
    _, R, C = w.shape
    ncols = C - col0 if ncols is None else ncols
    rows_out = R if rows_out is None else rows_out
    cols_out = ncols if cols_out is None else cols_out
    tr, tc = min(tr, rows_out), min(tc, cols_out)
    assert col0 % tc == 0
    c0 = col0 // tc
    gr, gc = pl.cdiv(rows_out, tr), pl.cdiv(cols_out, tc)
    rows = R if gr * tr > R else None
    cols = ncols if (gc * tc > ncols and cols_out > ncols) else None
    return pl.pallas_call(
        functools.partial(_cast_kernel, rows=rows, cols=cols),
        grid=(gr, gc),
        in_specs=[pl.BlockSpec((None, tr, tc), lambda i, j: (l, i, j + c0))],
        out_specs=pl.BlockSpec((tr, tc), lambda i, j: (i, j)),
        out_shape=jax.ShapeDtypeStruct((rows_out, cols_out), BF16),
        compiler_params=_params(("parallel", "parallel")),
        name="cast_bf16",
    )(w)


def _cast_t_kernel(x_ref, o_ref, *, rows):
    x = x_ref[0]
    if rows is not None:
        r = pl.program_id(1) * x.shape[0] + lax.broadcasted_iota(jnp.int32, x.shape, 0)
        x = jnp.where(r < rows, x, 0.0)
    o_ref[...] = x.T.astype(o_ref.dtype)


def _cast_t(wt, l, row0, nrows, cols_out=None, tk=1024, tc=512):
    _, N, K = wt.shape
    cols_out = nrows if cols_out is None else cols_out
    tk, tc = min(tk, K), min(tc, cols_out)
    assert row0 % 8 == 0 and K % tk == 0 and cols_out % tc == 0
    return pl.pallas_call(
        functools.partial(_cast_t_kernel, rows=nrows if cols_out > nrows else None),
        grid=(K // tk, cols_out // tc),
        in_specs=[pl.BlockSpec((pl.Element(1), pl.Element(tc), pl.Element(tk)),
                               lambda i, j: (l, pl.multiple_of(row0 + j * tc, 8), pl.multiple_of(i * tk, LANES)))],
        out_specs=pl.BlockSpec((tk, tc), lambda i, j: (i, j)),
        out_shape=jax.ShapeDtypeStruct((K, cols_out), BF16),
        compiler_params=_params(("parallel", "parallel")),
        name="cast_bf16_t",
    )(wt)


def _ffn_up_kernel(h_ref, wg_ref, wu_ref, o_ref):
    h = h_ref[...]
    g = jnp.dot(h, wg_ref[...], preferred_element_type=F32)
    u = jnp.dot(h, wu_ref[...], preferred_element_type=F32)
    o_ref[...] = (g * jax.nn.sigmoid(g) * u).astype(o_ref.dtype)


def _ffn_up(h, wg, wu, tm=1024, tn=512):
    T, D = h.shape
    N = wg.shape[1]
    tm, tn = _tile(T, tm), _tile(N, tn)
    return pl.pallas_call(
        _ffn_up_kernel,
        grid=(T // tm, N // tn),
        in_specs=[pl.BlockSpec((tm, D), lambda i, j: (i, 0)),
                  pl.BlockSpec((D, tn), lambda i, j: (0, j)),
                  pl.BlockSpec((D, tn), lambda i, j: (0, j))],
        out_specs=pl.BlockSpec((tm, tn), lambda i, j: (i, j)),
        out_shape=jax.ShapeDtypeStruct((T, N), BF16),
        compiler_params=_params(("parallel", "parallel")),
        name="ffn_up",
    )(h, wg, wu)


def _mm_res_kernel(*refs, scale, nk, n_x, na):
    a_ref, b_ref = refs[:2]
    x_refs, o_refs = refs[2:2 + n_x], refs[2 + n_x:]
    i, k = pl.program_id(0), pl.program_id(2)

    def part():
        return jnp.dot(a_ref[...], b_ref[...], preferred_element_type=F32)

    def accumulate(p):
        x_ref, o_ref = x_refs[min(p, n_x - 1)], o_refs[min(p, len(o_refs) - 1)]
        if nk == 1:
            o_ref[...] = x_ref[...] + scale * part()
            return

        @pl.when(k == 0)
        def _():
            o_ref[...] = part()

        @pl.when((k > 0) & (k < nk - 1))
        def _():
            o_ref[...] += part()

        @pl.when(k == nk - 1)
        def _():
            o_ref[...] = x_ref[...] + scale * (o_ref[...] + part())

    _for_piece(i, na, max(n_x, len(o_refs)), accumulate)


def _mm_res(a, b, xs, scale, out_rows=None, tm=1024, tn=1024, tk=1024):
    T, K = a.shape
    N = b.shape[1]
    out_rows = (T,) if out_rows is None else out_rows
    tm = _tile(math.gcd(*[x.shape[0] for x in xs], *out_rows), tm)
    tn, tk = _tile(N, tn), _tile(K, tk)
    nk, nj = K // tk, N // tn
    first = xs[0].shape[0] if len(xs) > 1 else out_rows[0]
    out = pl.pallas_call(
        functools.partial(_mm_res_kernel, scale=scale, nk=nk, n_x=len(xs), na=first // tm),
        grid=(T // tm, nj, nk),
        in_specs=[pl.BlockSpec((tm, tk), lambda i, j, k: (i, k)),
                  pl.BlockSpec((tk, tn), lambda i, j, k: (k, j))] + _piece_specs(len(xs), first, tm, tn, nj, True),
        out_specs=_piece_specs(len(out_rows), first, tm, tn, nj, True),
        out_shape=[jax.ShapeDtypeStruct((r, N), F32) for r in out_rows],
        compiler_params=_params(("parallel", "parallel", "arbitrary") if len(out_rows) == 1 else ("arbitrary",) * 3),
        name="mm_residual",
    )(a, b, *xs)
    return tuple(out)


def _rope128(y, cos, sin_signed):
    return y * cos + pltpu.roll(y, LANES // 2, 1) * sin_signed


def _norm_rope128(a, g, cos, sin_signed):
    return _rope128(a * lax.rsqrt(jnp.mean(a * a, axis=-1, keepdims=True) + RMS_EPS) * g, cos, sin_signed)


def _proj_rope_kernel(u_ref, w_ref, g_ref, cos_ref, sin_ref, o_ref, *, row_chunks):
    tm = u_ref.shape[0] // row_chunks
    for r in range(row_chunks):
        rows = slice(r * tm, (r + 1) * tm)
        acc = jnp.dot(u_ref[rows, :], w_ref[...], preferred_element_type=F32)
        for c in range(acc.shape[1] // LANES):
            sl = slice(c * LANES, (c + 1) * LANES)
            o_ref[rows, sl] = _norm_rope128(acc[:, sl], g_ref[:, sl], cos_ref[rows, :],
                                            sin_ref[rows, :]).astype(o_ref.dtype)


def _proj_rope(u, w, gains, cos, sin, seq, tm=1024, tn=256):
    T, D = u.shape
    N = w.shape[1]
    tm, tn = _tile(seq, tm), _tile(N, tn)
    npos = seq // tm
    return pl.pallas_call(
        functools.partial(_proj_rope_kernel, row_chunks=4 if tm % 1024 == 0 else 1),
        grid=(T // tm, N // tn),
        in_specs=[pl.BlockSpec((tm, D), lambda i, j: (i, 0)),
                  pl.BlockSpec((D, tn), lambda i, j: (0, j)),
                  pl.BlockSpec((1, tn), lambda i, j: (0, j)),
                  pl.BlockSpec((tm, LANES), lambda i, j: (i % npos, 0)),
                  pl.BlockSpec((tm, LANES), lambda i, j: (i % npos, 0))],
        out_specs=pl.BlockSpec((tm, tn), lambda i, j: (i, j)),
        out_shape=jax.ShapeDtypeStruct((T, N), BF16),
        compiler_params=_params(("parallel", "parallel")),
        name="proj_rope",
    )(u, w, gains, cos, sin)


def _proj_vt_kernel(u_ref, w_ref, o_ref):
    acc = jnp.dot(u_ref[...], w_ref[...], preferred_element_type=F32)
    tk = o_ref.shape[3]
    for c in range(o_ref.shape[1]):
        o_ref[0, c] = acc[c * tk:(c + 1) * tk, :].T.astype(o_ref.dtype)


def _proj_vt(u, w, heads, tk, tm=1024):
    T, D = u.shape
    hw = w.shape[1] // heads
    tm = _tile(T, tm)
    tk = _tile(tm, tk)
    return pl.pallas_call(
        _proj_vt_kernel,
        grid=(T // tm, heads),
        in_specs=[pl.BlockSpec((tm, D), lambda i, h: (i, 0)), pl.BlockSpec((D, hw), lambda i, h: (0, h))],
        out_specs=pl.BlockSpec((1, tm // tk, hw, tk), lambda i, h: (h, i, 0, 0)),
        out_shape=jax.ShapeDtypeStruct((heads, T // tk, hw, tk), BF16),
        compiler_params=_params(("parallel", "parallel")),
        name="proj_vt",
    )(u, w)


def _proj_b_kernel(u_ref, w_ref, g_ref, cos_ref, sin_ref, o_ref, scr, *, dil):
    t = pl.program_id(1)
    heads = [slice(h * B_DH, (h + 1) * B_DH) for h in range(B_HPG)]
    tm = u_ref.shape[0]
    row_chunks = 4 if tm % 1024 == 0 else 1
    chunks = [slice(r * (tm // row_chunks), (r + 1) * (tm // row_chunks)) for r in range(row_chunks)]

    @pl.when(t < 2)
    def _():
        for rows in chunks:
            acc = jnp.dot(u_ref[rows, :], w_ref[...], preferred_element_type=F32)
            for h, sl in enumerate(heads):
                scr[h, rows, :] = _norm_rope128(acc[:, sl], g_ref[:, sl], cos_ref[rows, :], sin_ref[rows, :])

    @pl.when(t == 2)
    def _():
        for rows in chunks:
            acc = jnp.dot(u_ref[rows, :], w_ref[...], preferred_element_type=F32)
            for h, sl in enumerate(heads):
                scr[h, rows, :] = acc[:, sl]

    n = tm // dil
    for c in range(dil):
        for h, sl in enumerate(heads):
            o_ref[c, :, sl] = scr[h, pl.ds(c, n, stride=dil), :].astype(o_ref.dtype)


def _proj_b(u, w_b, gains, cos, sin, g, nb, seq, tm=1024):
    T, D = u.shape
    dil = B_GROUPS[g][1]
    tm = _tile(seq, tm)
    assert tm % dil == 0 and (tm // dil) % 16 == 0
    npos = seq // tm
    ng = len(B_GROUPS)
    return pl.pallas_call(
        functools.partial(_proj_b_kernel, dil=dil),
        grid=(T // tm, 3),
        in_specs=[pl.BlockSpec((tm, D), lambda i, t: (i, 0)),
                  pl.BlockSpec((D, B_GW), lambda i, t: (0, t * ng + g)),
                  pl.BlockSpec((1, B_GW), lambda i, t: (0, t)),
                  pl.BlockSpec((tm, LANES), lambda i, t: (i % npos, 0)),
                  pl.BlockSpec((tm, LANES), lambda i, t: (i % npos, 0))],
        out_specs=pl.BlockSpec((dil, tm // dil, B_GW), lambda i, t: (i // npos, i % npos, t)),
        out_shape=jax.ShapeDtypeStruct((nb * dil, seq // dil, 3 * B_GW), BF16),
        scratch_shapes=[pltpu.VMEM((B_HPG, tm, B_DH), F32)],
        compiler_params=_params(("parallel", "arbitrary")),
        name=f"proj_b_d{dil}",
    )(u, w_b, gains, cos, sin)


def _mm_kernel(a_ref, b_ref, o_ref):
    o_ref[...] = jnp.dot(a_ref[...], b_ref[...], preferred_element_type=F32).astype(o_ref.dtype)


def _mm(a, b, out_dtype, tm=1024, tn=256):
    T, K = a.shape
    N = b.shape[1]
    tm, tn = _tile(T, tm), _tile(N, tn)
    return pl.pallas_call(
        _mm_kernel,
        grid=(T // tm, N // tn),
        in_specs=[pl.BlockSpec((tm, K), lambda i, j: (i, 0)), pl.BlockSpec((K, tn), lambda i, j: (0, j))],
        out_specs=pl.BlockSpec((tm, tn), lambda i, j: (i, j)),
        out_shape=jax.ShapeDtypeStruct((T, N), out_dtype),
        compiler_params=_params(("parallel", "parallel")),
        name="mm",
    )(a, b)


def _rope64_in128(y, cos, sin_lo, sin_hi):
    q = C_ROPE // 2
    return y * cos + pltpu.roll(y, LANES - q, 1) * sin_lo + pltpu.roll(y, q, 1) * sin_hi


def _mla_q_kernel(cq_ref, ng_ref, w_ref, g_ref, cos_ref, slo_ref, shi_ref, o_ref, a_scr, *, tn):
    @pl.when(pl.program_id(1) == 0)
    def _():
        x = cq_ref[...]
        y = x * lax.rsqrt(jnp.mean(x * x, axis=-1, keepdims=True) + RMS_EPS) * ng_ref[...]
        a_scr[...] = y.astype(a_scr.dtype)

    tm = a_scr.shape[0]
    row_chunks = 2 if tm % 512 == 0 else 1
    for rc in range(row_chunks):
        rows = slice(rc * (tm // row_chunks), (rc + 1) * (tm // row_chunks))
        acc = jnp.dot(a_scr[rows, :], w_ref[...], preferred_element_type=F32)
        for h in range(tn // C_HPAD):
            a = acc[:, h * C_HPAD:(h + 1) * C_HPAD]
            r = lax.rsqrt(jnp.sum(a * a, axis=-1, keepdims=True) * (1.0 / C_DQK) + RMS_EPS)
            y = a * r * g_ref[...]
            o_ref[rows, h * C_HPAD:h * C_HPAD + LANES] = y[:, :LANES].astype(o_ref.dtype)
            o_ref[rows, h * C_HPAD + LANES:(h + 1) * C_HPAD] = _rope64_in128(
                y[:, LANES:], cos_ref[rows, :], slo_ref[rows, :], shi_ref[rows, :]).astype(o_ref.dtype)


def _mla_q(pc, qa_gain, wq_pad, q_gain_pad, cos, slo, shi, seq, tm=512, tn=512):
    T = pc.shape[0]
    N = wq_pad.shape[1]
    tm, tn = _tile(seq, tm), _tile(N, tn)
    npos = seq // tm
    pos = lambda i, j: (i % npos, 0)
    return pl.pallas_call(
        functools.partial(_mla_q_kernel, tn=tn),
        grid=(T // tm, N // tn),
        in_specs=[pl.BlockSpec((tm, C_Q_RANK), lambda i, j: (i, 0)),
                  pl.BlockSpec((1, C_Q_RANK), lambda i, j: (0, 0)),
                  pl.BlockSpec((C_Q_RANK, tn), lambda i, j: (0, j)),
                  pl.BlockSpec((1, C_HPAD), lambda i, j: (0, 0)),
                  pl.BlockSpec((tm, LANES), pos), pl.BlockSpec((tm, LANES), pos), pl.BlockSpec((tm, LANES), pos)],
        out_specs=pl.BlockSpec((tm, tn), lambda i, j: (i, j)),
        out_shape=jax.ShapeDtypeStruct((T, N), BF16),
        scratch_shapes=[pltpu.VMEM((tm, C_Q_RANK), BF16)],
        compiler_params=_params(("parallel", "arbitrary")),
        name="mla_q_up",
    )(pc, qa_gain, wq_pad, q_gain_pad, cos, slo, shi)


def _mla_kv_kernel(ckv_ref, ng_ref, kr_ref, w_ref, gn_ref, gr_ref, cos_ref, slo_ref, shi_ref,
                   k_ref, vt_ref, a_scr, *, heads):
    @pl.when(pl.program_id(1) == 0)
    def _():
        x = ckv_ref[...]
        y = x * lax.rsqrt(jnp.mean(x * x, axis=-1, keepdims=True) + RMS_EPS) * ng_ref[...]
        a_scr[...] = y.astype(a_scr.dtype)

    acc = jnp.dot(a_scr[...], w_ref[...], preferred_element_type=F32)
    kr = kr_ref[...]
    kr_ss = jnp.sum(kr * kr, axis=-1, keepdims=True)
    tk = vt_ref.shape[3]
    for h in range(heads):
        base = h * (C_NOPE + C_VDIM)
        kn = acc[:, base:base + C_NOPE]
        r = lax.rsqrt((jnp.sum(kn * kn, axis=-1, keepdims=True) + kr_ss) * (1.0 / C_DQK) + RMS_EPS)
        k_ref[:, h * C_HPAD:h * C_HPAD + LANES] = (kn * r * gn_ref[...]).astype(k_ref.dtype)
        k_ref[:, h * C_HPAD + LANES:(h + 1) * C_HPAD] = _rope64_in128(
            kr * r * gr_ref[...], cos_ref[...], slo_ref[...], shi_ref[...]).astype(k_ref.dtype)
        vt = acc[:, base + C_NOPE:base + C_NOPE + C_VDIM].T
        extra = C_VT_ROWS - C_VDIM
        ones_row = jnp.where(lax.broadcasted_iota(jnp.int32, (extra, tk), 0) == 0, 1.0, 0.0)
        for c in range(vt_ref.shape[1]):
            vt_ref[h, c, :C_VDIM, :] = vt[:, c * tk:(c + 1) * tk].astype(vt_ref.dtype)
            vt_ref[h, c, C_VDIM:, :] = ones_row.astype(vt_ref.dtype)


def _mla_kv(pc, kva_gain, wkv, k_gain_nope, k_gain_rope, cos, slo, shi, seq, tk, tm=512, heads=2):
    T = pc.shape[0]
    tm = _tile(seq, tm)
    tk = _tile(tm, tk)
    npos = seq // tm
    pos = lambda i, j: (i % npos, 0)
    hw = C_NOPE + C_VDIM
    return pl.pallas_call(
        functools.partial(_mla_kv_kernel, heads=heads),
        grid=(T // tm, C_HEADS // heads),
        in_specs=[pl.BlockSpec((tm, C_KV_RANK), lambda i, j: (i, C_Q_RANK // C_KV_RANK)),
                  pl.BlockSpec((1, C_KV_RANK), lambda i, j: (0, 0)),
                  pl.BlockSpec((tm, LANES), lambda i, j: (i, (C_Q_RANK + C_KV_RANK) // LANES)),
                  pl.BlockSpec((C_KV_RANK, heads * hw), lambda i, j: (0, j)),
                  pl.BlockSpec((1, LANES), lambda i, j: (0, 0)),
                  pl.BlockSpec((1, LANES), lambda i, j: (0, 0)),
                  pl.BlockSpec((tm, LANES), pos), pl.BlockSpec((tm, LANES), pos), pl.BlockSpec((tm, LANES), pos)],
        out_specs=[pl.BlockSpec((tm, heads * C_HPAD), lambda i, j: (i, j)),
                   pl.BlockSpec((heads, tm // tk, C_VT_ROWS, tk), lambda i, j: (j, i, 0, 0))],
        out_shape=[jax.ShapeDtypeStruct((T, C_HEADS * C_HPAD), BF16),
                   jax.ShapeDtypeStruct((C_HEADS, T // tk, C_VT_ROWS, tk), BF16)],
        scratch_shapes=[pltpu.VMEM((tm, C_KV_RANK), BF16)],
        compiler_params=_params(("parallel", "arbitrary")),
        name="mla_kv_up",
    )(pc, kva_gain, pc, wkv, k_gain_nope, k_gain_rope, cos, slo, shi)


def _nt_dot(a, b):
    return lax.dot_general(a, b, (((1,), (1,)), ((), ())), preferred_element_type=F32)


def _scores_t(k, q, s_ref):
    s = _nt_dot(k, q)
    s_ref[...] = s
    return jnp.max(s, axis=0, keepdims=True)


def _softmax_step_t(s_ref, s_max, vt, m_prev, l_prev, acc_ref, idx):
    m_new = jnp.maximum(m_prev, s_max)
    alpha = jnp.exp2(m_prev - m_new)
    p = jnp.exp2(s_ref[...] - m_new)
    l_new = None if l_prev is None else alpha * l_prev + jnp.sum(p, axis=0, keepdims=True)
    acc_ref[idx] = acc_ref[idx] * alpha + jnp.dot(vt, p.astype(vt.dtype), preferred_element_type=F32)
    return m_new, l_new


def _pipelined_key_loop(nk, scores, consume, init):
    assert nk % 2 == 0

    def pair(jj, carry):
        state, smax0 = carry
        j = 2 * jj
        smax1 = scores(1, j + 1)
        state = consume(0, j, smax0, state)
        smax0 = scores(0, j + 2)
        state = consume(1, j + 1, smax1, state)
        return state, smax0

    state, smax0 = lax.fori_loop(0, nk // 2 - 1, pair, (init, scores(0, 0)))
    smax1 = scores(1, nk - 1)
    state = consume(0, nk - 2, smax0, state)
    return consume(1, nk - 1, smax1, state)


def _diff_attn_kernel(q_ref, k_ref, vt_ref, lq1_ref, lk1_ref, lq2_ref, lk2_ref, g_ref, o_ref, acc_ref, s_ref,
                      *, lam_init):
    tq = q_ref.shape[0]
    tk = vt_ref.shape[3]
    nk = k_ref.shape[0] // tk
    q = q_ref[...]
    acc_ref[...] = jnp.zeros_like(acc_ref)
    subs = [slice(sub * A_DH, (sub + 1) * A_DH) for sub in range(2)]

    def scores(slot, j):
        k = k_ref[pl.ds(pl.multiple_of(j * tk, tk), tk), :]
        return tuple(_scores_t(k[:, sl], q[:, sl], s_ref.at[slot, sub]) for sub, sl in enumerate(subs))

    def consume(slot, j, smax, state):
        vt = vt_ref[0, j]
        m0, l0, m1, l1 = state
        m0, l0 = _softmax_step_t(s_ref.at[slot, 0], smax[0], vt, m0, l0, acc_ref, 0)
        m1, l1 = _softmax_step_t(s_ref.at[slot, 1], smax[1], vt, m1, l1, acc_ref, 1)
        return m0, l0, m1, l1

    neg = jnp.full((1, tq), -jnp.inf, F32)
    zero = jnp.zeros((1, tq), F32)
    _, l0, _, l1 = _pipelined_key_loop(nk, scores, consume, (neg, zero, neg, zero))

    lam = (jnp.exp(jnp.sum(lq1_ref[...] * lk1_ref[...], axis=-1, keepdims=True))
           - jnp.exp(jnp.sum(lq2_ref[...] * lk2_ref[...], axis=-1, keepdims=True)) + lam_init)
    o = (acc_ref[0] * (1.0 / l0) - lam * (acc_ref[1] * (1.0 / l1))).T
    y = o * lax.rsqrt(jnp.mean(o * o, axis=-1, keepdims=True) + RMS_EPS) * g_ref[...]
    o_ref[...] = (y * (1.0 - lam_init)).astype(o_ref.dtype)


def _diff_attn(qk, vt, lq1, lk1, lq2, lk2, subln, nb, seq, lam_init, tq=1024):
    T = qk.shape[0]
    tk = vt.shape[3]
    tq = _tile(seq, tq)
    nq = seq // tq
    small = pl.BlockSpec((1, A_DH), lambda b, h, i: (0, 0))
    return pl.pallas_call(
        functools.partial(_diff_attn_kernel, lam_init=lam_init),
        grid=(nb, A_HEADS, nq),
        in_specs=[pl.BlockSpec((tq, A_HW), lambda b, h, i: (b * nq + i, h)),
                  pl.BlockSpec((seq, A_HW), lambda b, h, i: (b, A_HEADS + h)),
                  pl.BlockSpec((1, seq // tk, A_HW, tk), lambda b, h, i: (h, b, 0, 0)),
                  small, small, small, small,
                  pl.BlockSpec((1, A_HW), lambda b, h, i: (0, 0))],
        out_specs=pl.BlockSpec((tq, A_HW), lambda b, h, i: (b * nq + i, h)),
        out_shape=jax.ShapeDtypeStruct((T, A_W), BF16),
        scratch_shapes=[pltpu.VMEM((2, A_HW, tq), F32), pltpu.VMEM((2, 2, tk, tq), F32)],
        compiler_params=_params(("parallel", "parallel", "parallel")),
        name="diff_attn",
    )(qk, qk, vt, _row(lq1), _row(lk1), _row(lq2), _row(lk2), _row(subln))


def _mla_attn_kernel(q_ref, k_ref, vt_ref, o_ref, acc_ref, s_ref):
    tq = q_ref.shape[0]
    tk = vt_ref.shape[3]
    nk = k_ref.shape[0] // tk
    q = q_ref[...]
    acc_ref[...] = jnp.zeros_like(acc_ref)

    def scores(slot, j):
        return _scores_t(k_ref[pl.ds(pl.multiple_of(j * tk, tk), tk), :], q, s_ref.at[slot])

    def consume(slot, j, smax, m):
        return _softmax_step_t(s_ref.at[slot], smax, vt_ref[0, j], m, None, acc_ref, 0)[0]

    _pipelined_key_loop(nk, scores, consume, jnp.full((1, tq), -jnp.inf, F32))
    l = acc_ref[0, C_VDIM:C_VDIM + 1, :]
    o_ref[...] = (acc_ref[0, :C_VDIM, :] * (1.0 / l)).T.astype(o_ref.dtype)


def _mla_attn(qc, kc, vt, nb, seq, tq=1024):
    T = qc.shape[0]
    tk = vt.shape[3]
    tq = _tile(seq, tq)
    nq = seq // tq
    return pl.pallas_call(
        _mla_attn_kernel,
        grid=(nb, C_HEADS, nq),
        in_specs=[pl.BlockSpec((tq, C_HPAD), lambda b, h, i: (b * nq + i, h)),
                  pl.BlockSpec((seq, C_HPAD), lambda b, h, i: (b, h)),
                  pl.BlockSpec((1, seq // tk, C_VT_ROWS, tk), lambda b, h, i: (h, b, 0, 0))],
        out_specs=pl.BlockSpec((tq, C_VDIM), lambda b, h, i: (b * nq + i, h)),
        out_shape=jax.ShapeDtypeStruct((T, C_OUT), BF16),
        scratch_shapes=[pltpu.VMEM((1, C_VT_ROWS, tq), F32), pltpu.VMEM((2, tk, tq), F32)],
        compiler_params=_params(("parallel", "parallel", "parallel")),
        name="mla_attn",
    )(qc, kc, vt)


def _window_attn_kernel(q_ref, kp_ref, km_ref, kn_ref, vp_ref, vm_ref, vn_ref, o_ref, lse_ref, *, half, sub_len):
    tq = q_ref.shape[0]
    i = pl.program_id(1)
    span = tq + 2 * half
    qpos = i * tq + lax.broadcasted_iota(jnp.int32, (tq, span), 0)
    kpos = i * tq - half + lax.broadcasted_iota(jnp.int32, (tq, span), 1)
    valid = (jnp.abs(kpos - qpos) <= half) & (kpos >= 0) & (kpos < sub_len)
    scale = B_DH ** -0.5
    for h in range(B_HPG):
        sl = slice(h * B_DH, (h + 1) * B_DH)
        k = jnp.concatenate([kp_ref[:, sl], km_ref[:, sl], kn_ref[:, sl]], axis=0)
        v = jnp.concatenate([vp_ref[:, sl], vm_ref[:, sl], vn_ref[:, sl]], axis=0)
        s = jnp.where(valid, _nt_dot(q_ref[:, sl], k) * scale, NEG_BIG)
        m = jnp.max(s, axis=-1, keepdims=True)
        p = jnp.exp(s - m)
        l = jnp.sum(p, axis=-1, keepdims=True)
        o = jnp.dot(p.astype(v.dtype), v, preferred_element_type=F32) / l
        o_ref[:, sl] = o.astype(o_ref.dtype)
        lse_ref[:, sl] = jnp.broadcast_to(m + jnp.log(l), (tq, B_DH))


def _window_attn(qkv, g, tq=256):
    win, dil = B_GROUPS[g]
    half = win // (2 * dil)
    nsub, L, _ = qkv.shape
    tq = _tile(L, tq)
    assert tq % half == 0
    nq = L // tq
    r = tq // half
    nh = L // half
    flat = qkv.reshape(nsub * L, 3 * B_GW)

    def main(col):
        return pl.BlockSpec((tq, B_GW), lambda s, i: (s * nq + i, col))

    def prev(col):
        return pl.BlockSpec((half, B_GW), lambda s, i: (s * nh + jnp.maximum(i * r - 1, 0), col))

    def nxt(col):
        return pl.BlockSpec((half, B_GW), lambda s, i: (s * nh + jnp.minimum((i + 1) * r, nh - 1), col))

    out_spec = pl.BlockSpec((tq, B_GW), lambda s, i: (s * nq + i, 0))
    o, lse = pl.pallas_call(
        functools.partial(_window_attn_kernel, half=half, sub_len=L),
        grid=(nsub, nq),
        in_specs=[main(0), prev(1), main(1), nxt(1), prev(2), main(2), nxt(2)],
        out_specs=[out_spec, out_spec],
        out_shape=[jax.ShapeDtypeStruct((nsub * L, B_GW), BF16), jax.ShapeDtypeStruct((nsub * L, B_GW), F32)],
        compiler_params=_params(("parallel", "parallel")),
        name=f"window_attn_d{dil}",
    )(flat, flat, flat, flat, flat, flat, flat)
    return o.reshape(nsub, L, B_GW), lse.reshape(nsub, L, B_GW)


def _combine_kernel(o0, o1, o2, l0, l1, l2, o_ref, so, sl):
    tm = o_ref.shape[0]
    heads = [slice(h * B_DH, (h + 1) * B_DH) for h in range(B_HPG)]
    for g, (og, lg) in enumerate(((o0, l0), (o1, l1), (o2, l2))):
        dil = og.shape[0]
        n = tm // dil
        for c in range(dil):
            for h, hs in enumerate(heads):
                so[g * B_HPG + h, pl.ds(c, n, stride=dil), :] = og[c, :, hs].astype(F32)
                sl[g * B_HPG + h, pl.ds(c, n, stride=dil), :] = lg[c, :, hs]
    for h, hs in enumerate(heads):
        a0, a1, a2 = sl[h], sl[B_HPG + h], sl[2 * B_HPG + h]
        m = jnp.maximum(jnp.maximum(a0, a1), a2)
        e0, e1, e2 = jnp.exp(a0 - m), jnp.exp(a1 - m), jnp.exp(a2 - m)
        num = e0 * so[h] + e1 * so[B_HPG + h] + e2 * so[2 * B_HPG + h]
        o_ref[:, hs] = (num / (e0 + e1 + e2)).astype(o_ref.dtype)


def _combine(outs, lses, nb, seq, tm=512):
    tm = _tile(seq, tm)
    npos = seq // tm
    ng = len(outs)

    def spec(a):
        dil = a.shape[0] // nb
        return pl.BlockSpec((dil, tm // dil, B_GW), lambda i: (i // npos, i % npos, 0))

    return pl.pallas_call(
        _combine_kernel,
        grid=(nb * npos,),
        in_specs=[spec(a) for a in outs] + [spec(a) for a in lses],
        out_specs=pl.BlockSpec((tm, B_GW), lambda i: (i, 0)),
        out_shape=jax.ShapeDtypeStruct((nb * seq, B_GW), BF16),
        scratch_shapes=[pltpu.VMEM((ng * B_HPG, tm, B_DH), F32), pltpu.VMEM((ng * B_HPG, tm, B_DH), F32)],
        compiler_params=_params(("parallel",)),
        name="window_combine",
    )(*outs, *lses)


def _merge_kernel(u_ref, oa_ref, ob_ref, oc_ref, wga_ref, wgb_ref, wgc_ref, wa_ref, wb_ref, wc_ref, o_ref):
    u = u_ref[...]

    def branch(wg_ref, x_ref, w_ref):
        gate = jax.nn.sigmoid(jnp.dot(u, wg_ref[...], preferred_element_type=F32))
        return gate * jnp.dot(x_ref[...], w_ref[...], preferred_element_type=F32)

    o_ref[...] = (branch(wga_ref, oa_ref, wa_ref) + branch(wgb_ref, ob_ref, wb_ref)
                  + branch(wgc_ref, oc_ref, wc_ref)).astype(o_ref.dtype)


def _merge(u, oa, ob, oc, w_gates, wa, wb, wc, tm=512, tn=256):
    T, D = u.shape
    tm, tn = _tile(T, tm), _tile(D, tn)
    nj = D // tn
    row = lambda w: pl.BlockSpec((tm, w), lambda i, j: (i, 0))
    col = lambda k: pl.BlockSpec((k, tn), lambda i, j: (0, j))
    gate = lambda n: pl.BlockSpec((D, tn), lambda i, j: (0, n * nj + j))
    return pl.pallas_call(
        _merge_kernel,
        grid=(T // tm, nj),
        in_specs=[row(D), row(oa.shape[1]), row(ob.shape[1]), row(oc.shape[1]),
                  gate(0), gate(1), gate(2), col(wa.shape[0]), col(wb.shape[0]), col(wc.shape[0])],
        out_specs=pl.BlockSpec((tm, tn), lambda i, j: (i, j)),
        out_shape=jax.ShapeDtypeStruct((T, D), BF16),
        compiler_params=_params(("parallel", "parallel")),
        name="gated_merge",
    )(u, oa, ob, oc, w_gates, w_gates, w_gates, wa, wb, wc)


def _rope_tables(seq):
    pos = jnp.arange(seq, dtype=F32)[:, None]

    def ang(d):
        half = d // 2
        inv = 1.0 / (ROPE_THETA ** (jnp.arange(half, dtype=F32) * (2.0 / d)))
        return pos * inv[None, :]

    a = ang(A_DH)
    cos128 = jnp.concatenate([jnp.cos(a), jnp.cos(a)], axis=-1)
    sin128 = jnp.concatenate([-jnp.sin(a), jnp.sin(a)], axis=-1)
    c = ang(C_ROPE)
    z = jnp.zeros_like(c)
    cos64 = jnp.concatenate([jnp.cos(c), jnp.cos(c), z, z], axis=-1)
    sin_lo = jnp.concatenate([-jnp.sin(c), z, z, z], axis=-1)
    sin_hi = jnp.concatenate([z, jnp.sin(c), z, z], axis=-1)
    return cos128, sin128, cos64, sin_lo, sin_hi


def _layer(x, l, nb, seq, tabs, p, out_rows):
    cos128, sin128, cos64, sin_lo, sin_hi = tabs
    D = x[0].shape[1]
    ff = p["ffn1_wg"].shape[2]
    ffp = -(-ff // 1024) * 1024

    def ffn(x, norm, wg, wu, wd, out_rows=None):
        h = _rmsnorm(x, norm[l])
        act = _ffn_up(h, _cast(wg, l, cols_out=ffp), _cast(wu, l, cols_out=ffp))
        return _mm_res(act, _cast(wd, l, rows_out=ffp), x, 0.5, out_rows, tk=ffp // 4)

    x = ffn(x, p["ffn1_norm"], p["ffn1_wg"], p["ffn1_wu"], p["ffn1_wd"])

    u = _rmsnorm(x, p["mix_norm"][l])
    w_in_t = jnp.swapaxes(p["w_in"], 1, 2)
    b0 = 3 * A_W
    tile_h = lambda g, n: jnp.tile(g, n)

    a_gains = jnp.concatenate([tile_h(p["a_qnorm"][l] * (A_DH ** -0.5 * LOG2E), A_W // A_DH),
                               tile_h(p["a_knorm"][l], A_W // A_DH)]).reshape(1, 2 * A_W)
    a_qk = _proj_rope(u, _cast_t(w_in_t, l, 0, 2 * A_W), a_gains, cos128, sin128, seq)
    a_vt = _proj_vt(u, _cast_t(w_in_t, l, 2 * A_W, A_W), A_HEADS, ATT_TK)
    lam_init = 0.8 - 0.6 * math.exp(-0.3 * l)
    oa = _diff_attn(a_qk, a_vt, p["a_lq1"][l], p["a_lk1"][l], p["a_lq2"][l], p["a_lk2"][l], p["a_subln"][l],
                    nb, seq, lam_init)

    b_gains = jnp.concatenate([tile_h(p["b_qnorm"][l], B_HPG), tile_h(p["b_knorm"][l], B_HPG),
                               jnp.ones((B_GW,), F32)]).reshape(1, 3 * B_GW)
    w_b = _cast_t(w_in_t, l, b0, 3 * B_W, tc=256)
    wres = [_window_attn(_proj_b(u, w_b, b_gains, cos128, sin128, g, nb, seq), g) for g in range(len(B_GROUPS))]
    ob = _combine([r[0] for r in wres], [r[1] for r in wres], nb, seq)

    pc = _mm(u, _cast_t(w_in_t, l, AB_COLS, C_COLS, cols_out=C_COLS_PAD, tc=256), F32)
    wq = p["c_wq_up"][l].reshape(C_Q_RANK, C_HEADS, C_DQK)
    wq_pad = jnp.pad(wq, ((0, 0), (0, 0), (0, C_HPAD - C_DQK))).reshape(C_Q_RANK, C_HEADS * C_HPAD).astype(BF16)
    qg = p["c_qnorm"][l] * (C_DQK ** -0.5 * LOG2E)
    kg = p["c_knorm"][l]
    pad_to = lambda v, n: jnp.pad(v, (0, n - v.shape[0]))
    qc = _mla_q(pc, _row(p["c_qa_norm"][l]), wq_pad, _row(pad_to(qg, C_HPAD)), cos64, sin_lo, sin_hi, seq)
    kc, c_vt = _mla_kv(pc, _row(p["c_kva_norm"][l]), p["c_wkv_up"][l].astype(BF16), _row(kg[:C_NOPE]),
                       _row(pad_to(kg[C_NOPE:], LANES)), cos64, sin_lo, sin_hi, seq, ATT_TK)
    oc = _mla_attn(qc, kc, c_vt, nb, seq)

    m = _merge(u, oa, ob, oc, _cast_t(w_in_t, l, AB_COLS + C_COLS, N_BRANCH * D), _cast(p["w_br_a"], l),
               _cast(p["w_br_b"], l), _cast(p["w_br_c"], l))
    x = _mm_res(m, _cast(p["w_out"], l), x, 1.0, tm=1024, tn=512, tk=D)

    return ffn(x, p["ffn2_norm"], p["ffn2_wg"], p["ffn2_wu"], p["ffn2_wd"], out_rows)


def kernel(x_prompt, x_sample, ffn1_norm, ffn1_wg, ffn1_wu, ffn1_wd, mix_norm, w_in, a_qnorm, a_knorm, a_lq1, a_lk1, a_lq2, a_lk2, a_subln, b_qnorm, b_knorm, c_qa_norm, c_kva_norm, c_wq_up, c_wkv_up, c_qnorm, c_knorm, w_br_a, w_br_b, w_br_c, w_out, ffn2_norm, ffn2_wg, ffn2_wu, ffn2_wd):
    p = dict(ffn1_norm=ffn1_norm, ffn1_wg=ffn1_wg, ffn1_wu=ffn1_wu, ffn1_wd=ffn1_wd, mix_norm=mix_norm, w_in=w_in,
             a_qnorm=a_qnorm, a_knorm=a_knorm, a_lq1=a_lq1, a_lk1=a_lk1, a_lq2=a_lq2, a_lk2=a_lk2, a_subln=a_subln,
             b_qnorm=b_qnorm, b_knorm=b_knorm, c_qa_norm=c_qa_norm, c_kva_norm=c_kva_norm, c_wq_up=c_wq_up,
             c_wkv_up=c_wkv_up, c_qnorm=c_qnorm, c_knorm=c_knorm, w_br_a=w_br_a, w_br_b=w_br_b, w_br_c=w_br_c,
             w_out=w_out, ffn2_norm=ffn2_norm, ffn2_wg=ffn2_wg, ffn2_wu=ffn2_wu, ffn2_wd=ffn2_wd)
    bp, seq, D = x_prompt.shape
    bs, seq_s, _ = x_sample.shape
    assert seq == seq_s
    nb = bp + bs
    x = (x_prompt.reshape(bp * seq, D), x_sample.reshape(bs * seq, D))
    tabs = _rope_tables(seq)
    depth = ffn1_norm.shape[0]
    for l in range(depth):
        x = _layer(x, l, nb, seq, tabs, p, (bp * seq, bs * seq) if l == depth - 1 else None)
    return x[0].reshape(bp, seq, D), x[1].reshape(bs, seq, D)
```

```python
import functools
import math

import jax
import jax.numpy as jnp
from jax import lax
from jax.experimental import pallas as pl
from jax.experimental.pallas import tpu as pltpu

F32 = jnp.float32
BF16 = jnp.bfloat16

RMS_EPS = 1e-6
ROPE_THETA = 10000.0
LOG2E = math.log2(math.e)
LANES = 128
V7X_VMEM_BYTES = 64 * 1024 * 1024
VMEM_LIMIT = V7X_VMEM_BYTES - 8 * 1024 * 1024

A_HEADS, A_DH = 8, 128
A_HW = 2 * A_DH
A_W = A_HEADS * A_HW
B_GROUPS = ((128, 1), (512, 4), (2048, 16))
B_HPG, B_DH = 6, 128
B_HEADS = B_HPG * len(B_GROUPS)
B_W = B_HEADS * B_DH
B_GW = B_HPG * B_DH
C_HEADS, C_Q_RANK, C_KV_RANK, C_NOPE, C_ROPE, C_VDIM = 16, 1024, 512, 128, 64, 128
C_DQK = C_NOPE + C_ROPE
C_HPAD = 2 * LANES
C_OUT = C_HEADS * C_VDIM
C_VT_ROWS = C_VDIM + 16
N_BRANCH = 3
AB_COLS = 3 * A_W + 3 * B_W
C_COLS = C_Q_RANK + C_KV_RANK + C_ROPE
C_COLS_PAD = 1792
NEG_BIG = -1e30
ATT_TK = 1024


def _tile(n, t):
    t = min(n, t)
    assert n % t == 0, (n, t)
    return t


def _params(sem):
    return pltpu.CompilerParams(dimension_semantics=sem, vmem_limit_bytes=VMEM_LIMIT)


def _row(a):
    return a.reshape(1, -1).astype(F32)


def _piece_specs(n, rows_first, tm, width, nj, two_d):
    if n == 1:
        return [pl.BlockSpec((tm, width), (lambda i, j, k: (i, j)) if two_d else (lambda i: (i, 0)))]
    na = rows_first // tm
    if two_d:
        first = lambda i, j, k: (jnp.minimum(i, na - 1), jnp.where(i < na, j, nj - 1))
        second = lambda i, j, k: (jnp.maximum(i - na, 0), jnp.where(i >= na, j, 0))
    else:
        first = lambda i: (jnp.minimum(i, na - 1), 0)
        second = lambda i: (jnp.maximum(i - na, 0), 0)
    return [pl.BlockSpec((tm, width), first), pl.BlockSpec((tm, width), second)]


def _for_piece(i, na, n, fn):
    if n == 1:
        fn(0)
    else:
        pl.when(i < na)(lambda: fn(0))
        pl.when(i >= na)(lambda: fn(1))


def _rmsnorm_kernel(*refs, na):
    x_refs, g_ref, o_ref = refs[:-2], refs[-2], refs[-1]

    def run(p):
        x = x_refs[p][...]
        y = x * lax.rsqrt(jnp.mean(x * x, axis=-1, keepdims=True) + RMS_EPS)
        o_ref[...] = (y * g_ref[...]).astype(o_ref.dtype)

    _for_piece(pl.program_id(0), na, len(x_refs), run)


def _rmsnorm(xs, g, tm=512):
    T, D = sum(x.shape[0] for x in xs), xs[0].shape[1]
    tm = _tile(math.gcd(*[x.shape[0] for x in xs]), tm)
    return pl.pallas_call(
        functools.partial(_rmsnorm_kernel, na=xs[0].shape[0] // tm),
        grid=(T // tm,),
        in_specs=_piece_specs(len(xs), xs[0].shape[0], tm, D, 1, False) + [pl.BlockSpec((1, D), lambda i: (0, 0))],
        out_specs=pl.BlockSpec((tm, D), lambda i: (i, 0)),
        out_shape=jax.ShapeDtypeStruct((T, D), BF16),
        compiler_params=_params(("parallel",)),
        name="rmsnorm",
    )(*xs, _row(g))


def _cast_kernel(x_ref, o_ref, *, rows, cols):
    x = x_ref[...]
    tr, tc = x.shape
    if rows is not None:
        r = pl.program_id(0) * tr + lax.broadcasted_iota(jnp.int32, (tr, tc), 0)
        x = jnp.where(r < rows, x, 0.0)
    if cols is not None:
        c = pl.program_id(1) * tc + lax.broadcasted_iota(jnp.int32, (tr, tc), 1)
        x = jnp.where(c < cols, x, 0.0)
    o_ref[...] = x.astype(o_ref.dtype)


def _cast(w, l, col0=0, ncols=None, rows_out=None, cols_out=None, tr=2048, tc=512):
    _, R, C = w.shape
    ncols = C - col0 if ncols is None else ncols
    rows_out = R if rows_out is None else rows_out
    cols_out = ncols if cols_out is None else cols_out
    tr, tc = min(tr, rows_out), min(tc, cols_out)
    assert col0 % tc == 0
    c0 = col0 // tc
    gr, gc = pl.cdiv(rows_out, tr), pl.cdiv(cols_out, tc)
    rows = R if gr * tr > R else None
    cols = ncols if (gc * tc > ncols and cols_out > ncols) else None
    return pl.pallas_call(
        functools.partial(_cast_kernel, rows=rows, cols=cols),
        grid=(gr, gc),
        in_specs=[pl.BlockSpec((None, tr, tc), lambda i, j: (l, i, j + c0))],
        out_specs=pl.BlockSpec((tr, tc), lambda i, j: (i, j)),
        out_shape=jax.ShapeDtypeStruct((rows_out, cols_out), BF16),
        compiler_params=_params(("parallel", "parallel")),
        name="cast_bf16",
    )(w)


def _cast_t_kernel(x_ref, o_ref, *, rows):
    x = x_ref[0]
    if rows is not None:
        r = pl.program_id(1) * x.shape[0] + lax.broadcasted_iota(jnp.int32, x.shape, 0)
        x = jnp.where(r < rows, x, 0.0)
    o_ref[...] = x.T.astype(o_ref.dtype)


def _cast_t(wt, l, row0, nrows, cols_out=None, tk=1024, tc=512):
    _, N, K = wt.shape
    cols_out = nrows if cols_out is None else cols_out
    tk, tc = min(tk, K), min(tc, cols_out)
    assert row0 % 8 == 0 and K % tk == 0 and cols_out % tc == 0
    return pl.pallas_call(
        functools.partial(_cast_t_kernel, rows=nrows if cols_out > nrows else None),
        grid=(K // tk, cols_out // tc),
        in_specs=[pl.BlockSpec((pl.Element(1), pl.Element(tc), pl.Element(tk)),
                               lambda i, j: (l, pl.multiple_of(row0 + j * tc, 8), pl.multiple_of(i * tk, LANES)))],
        out_specs=pl.BlockSpec((tk, tc), lambda i, j: (i, j)),
        out_shape=jax.ShapeDtypeStruct((K, cols_out), BF16),
        compiler_params=_params(("parallel", "parallel")),
        name="cast_bf16_t",
    )(wt)


def _ffn_up_kernel(h_ref, wg_ref, wu_ref, o_ref):
    h = h_ref[...]
    g = jnp.dot(h, wg_ref[...], preferred_element_type=F32)
    u = jnp.dot(h, wu_ref[...], preferred_element_type=F32)
    o_ref[...] = (g * jax.nn.sigmoid(g) * u).astype(o_ref.dtype)


def _ffn_up(h, wg, wu, tm=1024, tn=512):
    T, D = h.shape
    N = wg.shape[1]
    tm, tn = _tile(T, tm), _tile(N, tn)
    return pl.pallas_call(
        _ffn_up_kernel,
        grid=(T // tm, N // tn),
        in_specs=[pl.BlockSpec((tm, D), lambda i, j: (i, 0)),
                  pl.BlockSpec((D, tn), lambda i, j: (0, j)),
                  pl.BlockSpec((D, tn), lambda i, j: (0, j))],
        out_specs=pl.BlockSpec((tm, tn), lambda i, j: (i, j)),
        out_shape=jax.ShapeDtypeStruct((T, N), BF16),
        compiler_params=_params(("parallel", "parallel")),
        name="ffn_up",
    )(h, wg, wu)


def _mm_res_kernel(*refs, scale, nk, n_x, na):
    a_ref, b_ref = refs[:2]
    x_refs, o_refs = refs[2:2 + n_x], refs[2 + n_x:]
    i, k = pl.program_id(0), pl.program_id(2)

    def part():
        return jnp.dot(a_ref[...], b_ref[...], preferred_element_type=F32)

    def accumulate(p):
        x_ref, o_ref = x_refs[min(p, n_x - 1)], o_refs[min(p, len(o_refs) - 1)]
        if nk == 1:
            o_ref[...] = x_ref[...] + scale * part()
            return

        @pl.when(k == 0)
        def _():
            o_ref[...] = part()

        @pl.when((k > 0) & (k < nk - 1))
        def _():
            o_ref[...] += part()

        @pl.when(k == nk - 1)
        def _():
            o_ref[...] = x_ref[...] + scale * (o_ref[...] + part())

    _for_piece(i, na, max(n_x, len(o_refs)), accumulate)


def _mm_res(a, b, xs, scale, out_rows=None, tm=1024, tn=1024, tk=1024):
    T, K = a.shape
    N = b.shape[1]
    out_rows = (T,) if out_rows is None else out_rows
    tm = _tile(math.gcd(*[x.shape[0] for x in xs], *out_rows), tm)
    tn, tk = _tile(N, tn), _tile(K, tk)
    nk, nj = K // tk, N // tn
    first = xs[0].shape[0] if len(xs) > 1 else out_rows[0]
    out = pl.pallas_call(
        functools.partial(_mm_res_kernel, scale=scale, nk=nk, n_x=len(xs), na=first // tm),
        grid=(T // tm, nj, nk),
        in_specs=[pl.BlockSpec((tm, tk), lambda i, j, k: (i, k)),
                  pl.BlockSpec((tk, tn), lambda i, j, k: (k, j))] + _piece_specs(len(xs), first, tm, tn, nj, True),
        out_specs=_piece_specs(len(out_rows), first, tm, tn, nj, True),
        out_shape=[jax.ShapeDtypeStruct((r, N), F32) for r in out_rows],
        compiler_params=_params(("parallel", "parallel", "arbitrary") if len(out_rows) == 1 else ("arbitrary",) * 3),
        name="mm_residual",
    )(a, b, *xs)
    return tuple(out)


def _rope128(y, cos, sin_signed):
    return y * cos + pltpu.roll(y, LANES // 2, 1) * sin_signed


def _norm_rope128(a, g, cos, sin_signed):
    return _rope128(a * lax.rsqrt(jnp.mean(a * a, axis=-1, keepdims=True) + RMS_EPS) * g, cos, sin_signed)


def _proj_rope_kernel(u_ref, w_ref, g_ref, cos_ref, sin_ref, o_ref, *, row_chunks):
    tm = u_ref.shape[0] // row_chunks
    for r in range(row_chunks):
        rows = slice(r * tm, (r + 1) * tm)
        acc = jnp.dot(u_ref[rows, :], w_ref[...], preferred_element_type=F32)
        for c in range(acc.shape[1] // LANES):
            sl = slice(c * LANES, (c + 1) * LANES)
            o_ref[rows, sl] = _norm_rope128(acc[:, sl], g_ref[:, sl], cos_ref[rows, :],
                                            sin_ref[rows, :]).astype(o_ref.dtype)


def _proj_rope(u, w, gains, cos, sin, seq, tm=1024, tn=256):
    T, D = u.shape
    N = w.shape[1]
    tm, tn = _tile(seq, tm), _tile(N, tn)
    npos = seq // tm
    return pl.pallas_call(
        functools.partial(_proj_rope_kernel, row_chunks=8 if tm % 1024 == 0 else 1),
        grid=(T // tm, N // tn),
        in_specs=[pl.BlockSpec((tm, D), lambda i, j: (i, 0)),
                  pl.BlockSpec((D, tn), lambda i, j: (0, j)),
                  pl.BlockSpec((1, tn), lambda i, j: (0, j)),
                  pl.BlockSpec((tm, LANES), lambda i, j: (i % npos, 0)),
                  pl.BlockSpec((tm, LANES), lambda i, j: (i % npos, 0))],
        out_specs=pl.BlockSpec((tm, tn), lambda i, j: (i, j)),
        out_shape=jax.ShapeDtypeStruct((T, N), BF16),
        compiler_params=_params(("parallel", "parallel")),
        name="proj_rope",
    )(u, w, gains, cos, sin)


def _proj_vt_kernel(u_ref, w_ref, o_ref):
    acc = jnp.dot(u_ref[...], w_ref[...], preferred_element_type=F32)
    tk = o_ref.shape[3]
    for c in range(o_ref.shape[1]):
        o_ref[0, c] = acc[c * tk:(c + 1) * tk, :].T.astype(o_ref.dtype)


def _proj_vt(u, w, heads, tk, tm=1024):
    T, D = u.shape
    hw = w.shape[1] // heads
    tm = _tile(T, tm)
    tk = _tile(tm, tk)
    return pl.pallas_call(
        _proj_vt_kernel,
        grid=(T // tm, heads),
        in_specs=[pl.BlockSpec((tm, D), lambda i, h: (i, 0)), pl.BlockSpec((D, hw), lambda i, h: (0, h))],
        out_specs=pl.BlockSpec((1, tm // tk, hw, tk), lambda i, h: (h, i, 0, 0)),
        out_shape=jax.ShapeDtypeStruct((heads, T // tk, hw, tk), BF16),
        compiler_params=_params(("parallel", "parallel")),
        name="proj_vt",
    )(u, w)


def _proj_b_kernel(u_ref, w_ref, g_ref, cos_ref, sin_ref, o_ref, scr, *, dil):
    t = pl.program_id(1)
    heads = [slice(h * B_DH, (h + 1) * B_DH) for h in range(B_HPG)]
    tm = u_ref.shape[0]
    row_chunks = 8 if tm % 1024 == 0 else 1
    chunks = [slice(r * (tm // row_chunks), (r + 1) * (tm // row_chunks)) for r in range(row_chunks)]

    @pl.when(t < 2)
    def _():
        for rows in chunks:
            acc = jnp.dot(u_ref[rows, :], w_ref[...], preferred_element_type=F32)
            for h, sl in enumerate(heads):
                scr[h, rows, :] = _norm_rope128(acc[:, sl], g_ref[:, sl], cos_ref[rows, :], sin_ref[rows, :])

    @pl.when(t == 2)
    def _():
        for rows in chunks:
            acc = jnp.dot(u_ref[rows, :], w_ref[...], preferred_element_type=F32)
            for h, sl in enumerate(heads):
                scr[h, rows, :] = acc[:, sl]

    n = tm // dil
    for c in range(dil):
        for h, sl in enumerate(heads):
            o_ref[c, :, sl] = scr[h, pl.ds(c, n, stride=dil), :].astype(o_ref.dtype)


def _proj_b(u, w_b, gains, cos, sin, g, nb, seq, tm=1024):
    T, D = u.shape
    dil = B_GROUPS[g][1]
    tm = _tile(seq, tm)
    assert tm % dil == 0 and (tm // dil) % 16 == 0
    npos = seq // tm
    ng = len(B_GROUPS)
    return pl.pallas_call(
        functools.partial(_proj_b_kernel, dil=dil),
        grid=(T // tm, 3),
        in_specs=[pl.BlockSpec((tm, D), lambda i, t: (i, 0)),
                  pl.BlockSpec((D, B_GW), lambda i, t: (0, t * ng + g)),
                  pl.BlockSpec((1, B_GW), lambda i, t: (0, t)),
                  pl.BlockSpec((tm, LANES), lambda i, t: (i % npos, 0)),
                  pl.BlockSpec((tm, LANES), lambda i, t: (i % npos, 0))],
        out_specs=pl.BlockSpec((dil, tm // dil, B_GW), lambda i, t: (i // npos, i % npos, t)),
        out_shape=jax.ShapeDtypeStruct((nb * dil, seq // dil, 3 * B_GW), BF16),
        scratch_shapes=[pltpu.VMEM((B_HPG, tm, B_DH), F32)],
        compiler_params=_params(("parallel", "arbitrary")),
        name=f"proj_b_d{dil}",
    )(u, w_b, gains, cos, sin)


def _mm_kernel(a_ref, b_ref, o_ref):
    o_ref[...] = jnp.dot(a_ref[...], b_ref[...], preferred_element_type=F32).astype(o_ref.dtype)


def _mm(a, b, out_dtype, tm=1024, tn=256):
    T, K = a.shape
    N = b.shape[1]
    tm, tn = _tile(T, tm), _tile(N, tn)
    return pl.pallas_call(
        _mm_kernel,
        grid=(T // tm, N // tn),
        in_specs=[pl.BlockSpec((tm, K), lambda i, j: (i, 0)), pl.BlockSpec((K, tn), lambda i, j: (0, j))],
        out_specs=pl.BlockSpec((tm, tn), lambda i, j: (i, j)),
        out_shape=jax.ShapeDtypeStruct((T, N), out_dtype),
        compiler_params=_params(("parallel", "parallel")),
        name="mm",
    )(a, b)


def _rope64_in128(y, cos, sin_lo, sin_hi):
    q = C_ROPE // 2
    return y * cos + pltpu.roll(y, LANES - q, 1) * sin_lo + pltpu.roll(y, q, 1) * sin_hi


def _mla_q_kernel(cq_ref, ng_ref, w_ref, g_ref, cos_ref, slo_ref, shi_ref, o_ref, a_scr, *, tn):
    @pl.when(pl.program_id(1) == 0)
    def _():
        x = cq_ref[...]
        y = x * lax.rsqrt(jnp.mean(x * x, axis=-1, keepdims=True) + RMS_EPS) * ng_ref[...]
        a_scr[...] = y.astype(a_scr.dtype)

    tm = a_scr.shape[0]
    row_chunks = tm // 256 if tm % 512 == 0 else 1
    for rc in range(row_chunks):
        rows = slice(rc * (tm // row_chunks), (rc + 1) * (tm // row_chunks))
        acc = jnp.dot(a_scr[rows, :], w_ref[...], preferred_element_type=F32)
        for h in range(tn // C_HPAD):
            a = acc[:, h * C_HPAD:(h + 1) * C_HPAD]
            r = lax.rsqrt(jnp.sum(a * a, axis=-1, keepdims=True) * (1.0 / C_DQK) + RMS_EPS)
            y = a * r * g_ref[...]
            o_ref[rows, h * C_HPAD:h * C_HPAD + LANES] = y[:, :LANES].astype(o_ref.dtype)
            o_ref[rows, h * C_HPAD + LANES:(h + 1) * C_HPAD] = _rope64_in128(
                y[:, LANES:], cos_ref[rows, :], slo_ref[rows, :], shi_ref[rows, :]).astype(o_ref.dtype)


def _mla_q(pc, qa_gain, wq_pad, q_gain_pad, cos, slo, shi, seq, tm=1024, tn=512):
    T = pc.shape[0]
    N = wq_pad.shape[1]
    tm, tn = _tile(seq, tm), _tile(N, tn)
    npos = seq // tm
    pos = lambda i, j: (i % npos, 0)
    return pl.pallas_call(
        functools.partial(_mla_q_kernel, tn=tn),
        grid=(T // tm, N // tn),
        in_specs=[pl.BlockSpec((tm, C_Q_RANK), lambda i, j: (i, 0)),
                  pl.BlockSpec((1, C_Q_RANK), lambda i, j: (0, 0)),
                  pl.BlockSpec((C_Q_RANK, tn), lambda i, j: (0, j)),
                  pl.BlockSpec((1, C_HPAD), lambda i, j: (0, 0)),
                  pl.BlockSpec((tm, LANES), pos), pl.BlockSpec((tm, LANES), pos), pl.BlockSpec((tm, LANES), pos)],
        out_specs=pl.BlockSpec((tm, tn), lambda i, j: (i, j)),
        out_shape=jax.ShapeDtypeStruct((T, N), BF16),
        scratch_shapes=[pltpu.VMEM((tm, C_Q_RANK), BF16)],
        compiler_params=_params(("parallel", "arbitrary")),
        name="mla_q_up",
    )(pc, qa_gain, wq_pad, q_gain_pad, cos, slo, shi)


def _mla_kv_kernel(ckv_ref, ng_ref, kr_ref, w_ref, gn_ref, gr_ref, cos_ref, slo_ref, shi_ref,
                   k_ref, vt_ref, a_scr, *, heads):
    @pl.when(pl.program_id(1) == 0)
    def _():
        x = ckv_ref[...]
        y = x * lax.rsqrt(jnp.mean(x * x, axis=-1, keepdims=True) + RMS_EPS) * ng_ref[...]
        a_scr[...] = y.astype(a_scr.dtype)

    tm, tk = a_scr.shape[0], vt_ref.shape[3]
    tr = math.gcd(256, tk)
    extra = C_VT_ROWS - C_VDIM
    ones_row = jnp.where(lax.broadcasted_iota(jnp.int32, (extra, tr), 0) == 0, 1.0, 0.0).astype(vt_ref.dtype)
    for r0 in range(0, tm, tr):
        rows = slice(r0, r0 + tr)
        c, cols = r0 // tk, slice(r0 % tk, r0 % tk + tr)
        acc = jnp.dot(a_scr[rows, :], w_ref[...], preferred_element_type=F32)
        kr = kr_ref[rows, :]
        kr_ss = jnp.sum(kr * kr, axis=-1, keepdims=True)
        for h in range(heads):
            base = h * (C_NOPE + C_VDIM)
            kn = acc[:, base:base + C_NOPE]
            r = lax.rsqrt((jnp.sum(kn * kn, axis=-1, keepdims=True) + kr_ss) * (1.0 / C_DQK) + RMS_EPS)
            k_ref[rows, h * C_HPAD:h * C_HPAD + LANES] = (kn * r * gn_ref[...]).astype(k_ref.dtype)
            k_ref[rows, h * C_HPAD + LANES:(h + 1) * C_HPAD] = _rope64_in128(
                kr * r * gr_ref[...], cos_ref[rows, :], slo_ref[rows, :], shi_ref[rows, :]).astype(k_ref.dtype)
            vt_ref[h, c, :C_VDIM, cols] = acc[:, base + C_NOPE:base + C_NOPE + C_VDIM].T.astype(vt_ref.dtype)
            vt_ref[h, c, C_VDIM:, cols] = ones_row


def _mla_kv(pc, kva_gain, wkv, k_gain_nope, k_gain_rope, cos, slo, shi, seq, tk, tm=1024, heads=2):
    T = pc.shape[0]
    tm = _tile(seq, tm)
    tk = _tile(tm, tk)
    npos = seq // tm
    pos = lambda i, j: (i % npos, 0)
    hw = C_NOPE + C_VDIM
    return pl.pallas_call(
        functools.partial(_mla_kv_kernel, heads=heads),
        grid=(T // tm, C_HEADS // heads),
        in_specs=[pl.BlockSpec((tm, C_KV_RANK), lambda i, j: (i, C_Q_RANK // C_KV_RANK)),
                  pl.BlockSpec((1, C_KV_RANK), lambda i, j: (0, 0)),
                  pl.BlockSpec((tm, LANES), lambda i, j: (i, (C_Q_RANK + C_KV_RANK) // LANES)),
                  pl.BlockSpec((C_KV_RANK, heads * hw), lambda i, j: (0, j)),
                  pl.BlockSpec((1, LANES), lambda i, j: (0, 0)),
                  pl.BlockSpec((1, LANES), lambda i, j: (0, 0)),
                  pl.BlockSpec((tm, LANES), pos), pl.BlockSpec((tm, LANES), pos), pl.BlockSpec((tm, LANES), pos)],
        out_specs=[pl.BlockSpec((tm, heads * C_HPAD), lambda i, j: (i, j)),
                   pl.BlockSpec((heads, tm // tk, C_VT_ROWS, tk), lambda i, j: (j, i, 0, 0))],
        out_shape=[jax.ShapeDtypeStruct((T, C_HEADS * C_HPAD), BF16),
                   jax.ShapeDtypeStruct((C_HEADS, T // tk, C_VT_ROWS, tk), BF16)],
        scratch_shapes=[pltpu.VMEM((tm, C_KV_RANK), BF16)],
        compiler_params=_params(("parallel", "arbitrary")),
        name="mla_kv_up",
    )(pc, kva_gain, pc, wkv, k_gain_nope, k_gain_rope, cos, slo, shi)


def _nt_dot(a, b):
    return lax.dot_general(a, b, (((1,), (1,)), ((), ())), preferred_element_type=F32)


def _scores_t(k, q, s_ref):
    s = _nt_dot(k, q)
    s_ref[...] = s
    return jnp.max(s, axis=0, keepdims=True)


def _softmax_step_t(s_ref, s_max, vt, m_prev, l_prev, acc_ref, idx):
    m_new = jnp.maximum(m_prev, s_max)
    alpha = jnp.exp2(m_prev - m_new)
    p = jnp.exp2(s_ref[...] - m_new)
    l_new = None if l_prev is None else alpha * l_prev + jnp.sum(p, axis=0, keepdims=True)
    acc_ref[idx] = acc_ref[idx] * alpha + jnp.dot(vt, p.astype(vt.dtype), preferred_element_type=F32)
    return m_new, l_new


def _pipelined_key_loop(nk, scores, consume, init):
    assert nk % 2 == 0

    def pair(jj, carry):
        state, smax0 = carry
        j = 2 * jj
        smax1 = scores(1, j + 1)
        state = consume(0, j, smax0, state)
        smax0 = scores(0, j + 2)
        state = consume(1, j + 1, smax1, state)
        return state, smax0

    state, smax0 = lax.fori_loop(0, nk // 2 - 1, pair, (init, scores(0, 0)))
    smax1 = scores(1, nk - 1)
    state = consume(0, nk - 2, smax0, state)
    return consume(1, nk - 1, smax1, state)


def _diff_attn_kernel(q_ref, k_ref, vt_ref, lq1_ref, lk1_ref, lq2_ref, lk2_ref, g_ref, o_ref, acc_ref, s_ref,
                      *, lam_init):
    tq = q_ref.shape[0]
    tk = vt_ref.shape[3]
    nk = k_ref.shape[0] // tk
    q = q_ref[...]
    acc_ref[...] = jnp.zeros_like(acc_ref)
    subs = [slice(sub * A_DH, (sub + 1) * A_DH) for sub in range(2)]

    def scores(slot, j):
        k = k_ref[pl.ds(pl.multiple_of(j * tk, tk), tk), :]
        return tuple(_scores_t(k[:, sl], q[:, sl], s_ref.at[slot, sub]) for sub, sl in enumerate(subs))

    def consume(slot, j, smax, state):
        vt = vt_ref[0, j]
        m0, l0, m1, l1 = state
        m0, l0 = _softmax_step_t(s_ref.at[slot, 0], smax[0], vt, m0, l0, acc_ref, 0)
        m1, l1 = _softmax_step_t(s_ref.at[slot, 1], smax[1], vt, m1, l1, acc_ref, 1)
        return m0, l0, m1, l1

    neg = jnp.full((1, tq), -jnp.inf, F32)
    zero = jnp.zeros((1, tq), F32)
    _, l0, _, l1 = _pipelined_key_loop(nk, scores, consume, (neg, zero, neg, zero))

    lam = (jnp.exp(jnp.sum(lq1_ref[...] * lk1_ref[...], axis=-1, keepdims=True))
           - jnp.exp(jnp.sum(lq2_ref[...] * lk2_ref[...], axis=-1, keepdims=True)) + lam_init)
    o = (acc_ref[0] * (1.0 / l0) - lam * (acc_ref[1] * (1.0 / l1))).T
    y = o * lax.rsqrt(jnp.mean(o * o, axis=-1, keepdims=True) + RMS_EPS) * g_ref[...]
    o_ref[...] = (y * (1.0 - lam_init)).astype(o_ref.dtype)


def _diff_attn(qk, vt, lq1, lk1, lq2, lk2, subln, nb, seq, lam_init, tq=1024):
    T = qk.shape[0]
    tk = vt.shape[3]
    tq = _tile(seq, tq)
    nq = seq // tq
    small = pl.BlockSpec((1, A_DH), lambda b, h, i: (0, 0))
    return pl.pallas_call(
        functools.partial(_diff_attn_kernel, lam_init=lam_init),
        grid=(nb, A_HEADS, nq),
        in_specs=[pl.BlockSpec((tq, A_HW), lambda b, h, i: (b * nq + i, h)),
                  pl.BlockSpec((seq, A_HW), lambda b, h, i: (b, A_HEADS + h)),
                  pl.BlockSpec((1, seq // tk, A_HW, tk), lambda b, h, i: (h, b, 0, 0)),
                  small, small, small, small,
                  pl.BlockSpec((1, A_HW), lambda b, h, i: (0, 0))],
        out_specs=pl.BlockSpec((tq, A_HW), lambda b, h, i: (b * nq + i, h)),
        out_shape=jax.ShapeDtypeStruct((T, A_W), BF16),
        scratch_shapes=[pltpu.VMEM((2, A_HW, tq), F32), pltpu.VMEM((2, 2, tk, tq), F32)],
        compiler_params=_params(("parallel", "parallel", "parallel")),
        name="diff_attn",
    )(qk, qk, vt, _row(lq1), _row(lk1), _row(lq2), _row(lk2), _row(subln))


def _mla_attn_kernel(q_ref, k_ref, vt_ref, o_ref, acc_ref, s_ref):
    tq = q_ref.shape[0]
    tk = vt_ref.shape[3]
    nk = k_ref.shape[0] // tk
    q = q_ref[...]
    acc_ref[...] = jnp.zeros_like(acc_ref)

    def scores(slot, j):
        return _scores_t(k_ref[pl.ds(pl.multiple_of(j * tk, tk), tk), :], q, s_ref.at[slot])

    def consume(slot, j, smax, m):
        return _softmax_step_t(s_ref.at[slot], smax, vt_ref[0, j], m, None, acc_ref, 0)[0]

    _pipelined_key_loop(nk, scores, consume, jnp.full((1, tq), -jnp.inf, F32))
    l = acc_ref[0, C_VDIM:C_VDIM + 1, :]
    o_ref[...] = (acc_ref[0, :C_VDIM, :] * (1.0 / l)).T.astype(o_ref.dtype)


def _mla_attn(qc, kc, vt, nb, seq, tq=1024):
    T = qc.shape[0]
    tk = vt.shape[3]
    tq = _tile(seq, tq)
    nq = seq // tq
    return pl.pallas_call(
        _mla_attn_kernel,
        grid=(nb, C_HEADS, nq),
        in_specs=[pl.BlockSpec((tq, C_HPAD), lambda b, h, i: (b * nq + i, h)),
                  pl.BlockSpec((seq, C_HPAD), lambda b, h, i: (b, h)),
                  pl.BlockSpec((1, seq // tk, C_VT_ROWS, tk), lambda b, h, i: (h, b, 0, 0))],
        out_specs=pl.BlockSpec((tq, C_VDIM), lambda b, h, i: (b * nq + i, h)),
        out_shape=jax.ShapeDtypeStruct((T, C_OUT), BF16),
        scratch_shapes=[pltpu.VMEM((1, C_VT_ROWS, tq), F32), pltpu.VMEM((2, tk, tq), F32)],
        compiler_params=_params(("parallel", "parallel", "parallel")),
        name="mla_attn",
    )(qc, kc, vt)


def _window_attn_kernel(q_ref, kp_ref, km_ref, kn_ref, vp_ref, vm_ref, vn_ref, o_ref, lse_ref, *, half, sub_len):
    tq = q_ref.shape[0]
    i = pl.program_id(1)
    span = tq + 2 * half
    qpos = i * tq + lax.broadcasted_iota(jnp.int32, (tq, span), 0)
    kpos = i * tq - half + lax.broadcasted_iota(jnp.int32, (tq, span), 1)
    valid = (jnp.abs(kpos - qpos) <= half) & (kpos >= 0) & (kpos < sub_len)
    scale = B_DH ** -0.5
    for h in range(B_HPG):
        sl = slice(h * B_DH, (h + 1) * B_DH)
        k = jnp.concatenate([kp_ref[:, sl], km_ref[:, sl], kn_ref[:, sl]], axis=0)
        v = jnp.concatenate([vp_ref[:, sl], vm_ref[:, sl], vn_ref[:, sl]], axis=0)
        s = jnp.where(valid, _nt_dot(q_ref[:, sl], k) * scale, NEG_BIG)
        m = jnp.max(s, axis=-1, keepdims=True)
        p = jnp.exp(s - m)
        l = jnp.sum(p, axis=-1, keepdims=True)
        o = jnp.dot(p.astype(v.dtype), v, preferred_element_type=F32) / l
        o_ref[:, sl] = o.astype(o_ref.dtype)
        lse_ref[:, sl] = jnp.broadcast_to(m + jnp.log(l), (tq, B_DH))


def _window_attn(qkv, g, tq=256):
    win, dil = B_GROUPS[g]
    half = win // (2 * dil)
    nsub, L, _ = qkv.shape
    tq = _tile(L, tq)
    assert tq % half == 0
    nq = L // tq
    r = tq // half
    nh = L // half
    flat = qkv.reshape(nsub * L, 3 * B_GW)

    def main(col):
        return pl.BlockSpec((tq, B_GW), lambda s, i: (s * nq + i, col))

    def prev(col):
        return pl.BlockSpec((half, B_GW), lambda s, i: (s * nh + jnp.maximum(i * r - 1, 0), col))

    def nxt(col):
        return pl.BlockSpec((half, B_GW), lambda s, i: (s * nh + jnp.minimum((i + 1) * r, nh - 1), col))

    out_spec = pl.BlockSpec((tq, B_GW), lambda s, i: (s * nq + i, 0))
    o, lse = pl.pallas_call(
        functools.partial(_window_attn_kernel, half=half, sub_len=L),
        grid=(nsub, nq),
        in_specs=[main(0), prev(1), main(1), nxt(1), prev(2), main(2), nxt(2)],
        out_specs=[out_spec, out_spec],
        out_shape=[jax.ShapeDtypeStruct((nsub * L, B_GW), BF16), jax.ShapeDtypeStruct((nsub * L, B_GW), F32)],
        compiler_params=_params(("parallel", "parallel")),
        name=f"window_attn_d{dil}",
    )(flat, flat, flat, flat, flat, flat, flat)
    return o.reshape(nsub, L, B_GW), lse.reshape(nsub, L, B_GW)


def _combine_kernel(o0, o1, o2, l0, l1, l2, o_ref, so, sl):
    tm = o_ref.shape[0]
    heads = [slice(h * B_DH, (h + 1) * B_DH) for h in range(B_HPG)]
    for g, (og, lg) in enumerate(((o0, l0), (o1, l1), (o2, l2))):
        dil = og.shape[0]
        n = tm // dil
        for c in range(dil):
            for h, hs in enumerate(heads):
                so[g * B_HPG + h, pl.ds(c, n, stride=dil), :] = og[c, :, hs].astype(F32)
                sl[g * B_HPG + h, pl.ds(c, n, stride=dil), :] = lg[c, :, hs]
    for h, hs in enumerate(heads):
        a0, a1, a2 = sl[h], sl[B_HPG + h], sl[2 * B_HPG + h]
        m = jnp.maximum(jnp.maximum(a0, a1), a2)
        e0, e1, e2 = jnp.exp(a0 - m), jnp.exp(a1 - m), jnp.exp(a2 - m)
        num = e0 * so[h] + e1 * so[B_HPG + h] + e2 * so[2 * B_HPG + h]
        o_ref[:, hs] = (num / (e0 + e1 + e2)).astype(o_ref.dtype)


def _combine(outs, lses, nb, seq, tm=512):
    tm = _tile(seq, tm)
    npos = seq // tm
    ng = len(outs)

    def spec(a):
        dil = a.shape[0] // nb
        return pl.BlockSpec((dil, tm // dil, B_GW), lambda i: (i // npos, i % npos, 0))

    return pl.pallas_call(
        _combine_kernel,
        grid=(nb * npos,),
        in_specs=[spec(a) for a in outs] + [spec(a) for a in lses],
        out_specs=pl.BlockSpec((tm, B_GW), lambda i: (i, 0)),
        out_shape=jax.ShapeDtypeStruct((nb * seq, B_GW), BF16),
        scratch_shapes=[pltpu.VMEM((ng * B_HPG, tm, B_DH), F32), pltpu.VMEM((ng * B_HPG, tm, B_DH), F32)],
        compiler_params=_params(("parallel",)),
        name="window_combine",
    )(*outs, *lses)


def _merge_kernel(u_ref, oa_ref, ob_ref, oc_ref, wga_ref, wgb_ref, wgc_ref, wa_ref, wb_ref, wc_ref, o_ref):
    u = u_ref[...]

    def branch(wg_ref, x_ref, w_ref):
        gate = jax.nn.sigmoid(jnp.dot(u, wg_ref[...], preferred_element_type=F32))
        return gate * jnp.dot(x_ref[...], w_ref[...], preferred_element_type=F32)

    o_ref[...] = (branch(wga_ref, oa_ref, wa_ref) + branch(wgb_ref, ob_ref, wb_ref)
                  + branch(wgc_ref, oc_ref, wc_ref)).astype(o_ref.dtype)


def _merge(u, oa, ob, oc, w_gates, wa, wb, wc, tm=512, tn=256):
    T, D = u.shape
    tm, tn = _tile(T, tm), _tile(D, tn)
    nj = D // tn
    row = lambda w: pl.BlockSpec((tm, w), lambda i, j: (i, 0))
    col = lambda k: pl.BlockSpec((k, tn), lambda i, j: (0, j))
    gate = lambda n: pl.BlockSpec((D, tn), lambda i, j: (0, n * nj + j))
    return pl.pallas_call(
        _merge_kernel,
        grid=(T // tm, nj),
        in_specs=[row(D), row(oa.shape[1]), row(ob.shape[1]), row(oc.shape[1]),
                  gate(0), gate(1), gate(2), col(wa.shape[0]), col(wb.shape[0]), col(wc.shape[0])],
        out_specs=pl.BlockSpec((tm, tn), lambda i, j: (i, j)),
        out_shape=jax.ShapeDtypeStruct((T, D), BF16),
        compiler_params=_params(("parallel", "parallel")),
        name="gated_merge",
    )(u, oa, ob, oc, w_gates, w_gates, w_gates, wa, wb, wc)


def _rope_tables(seq):
    pos = jnp.arange(seq, dtype=F32)[:, None]

    def ang(d):
        half = d // 2
        inv = 1.0 / (ROPE_THETA ** (jnp.arange(half, dtype=F32) * (2.0 / d)))
        return pos * inv[None, :]

    a = ang(A_DH)
    cos128 = jnp.concatenate([jnp.cos(a), jnp.cos(a)], axis=-1)
    sin128 = jnp.concatenate([-jnp.sin(a), jnp.sin(a)], axis=-1)
    c = ang(C_ROPE)
    z = jnp.zeros_like(c)
    cos64 = jnp.concatenate([jnp.cos(c), jnp.cos(c), z, z], axis=-1)
    sin_lo = jnp.concatenate([-jnp.sin(c), z, z, z], axis=-1)
    sin_hi = jnp.concatenate([z, jnp.sin(c), z, z], axis=-1)
    return cos128, sin128, cos64, sin_lo, sin_hi


def _layer(x, l, nb, seq, tabs, p, out_rows):
    cos128, sin128, cos64, sin_lo, sin_hi = tabs
    D = x[0].shape[1]
    ff = p["ffn1_wg"].shape[2]
    ffp = -(-ff // 1024) * 1024

    def ffn(x, norm, wg, wu, wd, out_rows=None):
        h = _rmsnorm(x, norm[l])
        act = _ffn_up(h, _cast(wg, l, cols_out=ffp), _cast(wu, l, cols_out=ffp))
        return _mm_res(act, _cast(wd, l, rows_out=ffp), x, 0.5, out_rows, tk=ffp // 4)

    x = ffn(x, p["ffn1_norm"], p["ffn1_wg"], p["ffn1_wu"], p["ffn1_wd"])

    u = _rmsnorm(x, p["mix_norm"][l])
    w_in_t = jnp.swapaxes(p["w_in"], 1, 2)
    b0 = 3 * A_W
    tile_h = lambda g, n: jnp.tile(g, n)

    a_gains = jnp.concatenate([tile_h(p["a_qnorm"][l] * (A_DH ** -0.5 * LOG2E), A_W // A_DH),
                               tile_h(p["a_knorm"][l], A_W // A_DH)]).reshape(1, 2 * A_W)
    a_qk = _proj_rope(u, _cast_t(w_in_t, l, 0, 2 * A_W), a_gains, cos128, sin128, seq)
    a_vt = _proj_vt(u, _cast_t(w_in_t, l, 2 * A_W, A_W), A_HEADS, ATT_TK)
    lam_init = 0.8 - 0.6 * math.exp(-0.3 * l)
    oa = _diff_attn(a_qk, a_vt, p["a_lq1"][l], p["a_lk1"][l], p["a_lq2"][l], p["a_lk2"][l], p["a_subln"][l],
                    nb, seq, lam_init)

    b_gains = jnp.concatenate([tile_h(p["b_qnorm"][l], B_HPG), tile_h(p["b_knorm"][l], B_HPG),
                               jnp.ones((B_GW,), F32)]).reshape(1, 3 * B_GW)
    w_b = _cast_t(w_in_t, l, b0, 3 * B_W, tc=256)
    wres = [_window_attn(_proj_b(u, w_b, b_gains, cos128, sin128, g, nb, seq), g) for g in range(len(B_GROUPS))]
    ob = _combine([r[0] for r in wres], [r[1] for r in wres], nb, seq)

    pc = _mm(u, _cast_t(w_in_t, l, AB_COLS, C_COLS, cols_out=C_COLS_PAD, tc=256), F32)
    wq = p["c_wq_up"][l].reshape(C_Q_RANK, C_HEADS, C_DQK)
    wq_pad = jnp.pad(wq, ((0, 0), (0, 0), (0, C_HPAD - C_DQK))).reshape(C_Q_RANK, C_HEADS * C_HPAD).astype(BF16)
    qg = p["c_qnorm"][l] * (C_DQK ** -0.5 * LOG2E)
    kg = p["c_knorm"][l]
    pad_to = lambda v, n: jnp.pad(v, (0, n - v.shape[0]))
    qc = _mla_q(pc, _row(p["c_qa_norm"][l]), wq_pad, _row(pad_to(qg, C_HPAD)), cos64, sin_lo, sin_hi, seq)
    kc, c_vt = _mla_kv(pc, _row(p["c_kva_norm"][l]), p["c_wkv_up"][l].astype(BF16), _row(kg[:C_NOPE]),
                       _row(pad_to(kg[C_NOPE:], LANES)), cos64, sin_lo, sin_hi, seq, ATT_TK)
    oc = _mla_attn(qc, kc, c_vt, nb, seq)

    m = _merge(u, oa, ob, oc, _cast_t(w_in_t, l, AB_COLS + C_COLS, N_BRANCH * D), _cast(p["w_br_a"], l),
               _cast(p["w_br_b"], l), _cast(p["w_br_c"], l))
    x = _mm_res(m, _cast(p["w_out"], l), x, 1.0, tm=1024, tn=512, tk=D)

    return ffn(x, p["ffn2_norm"], p["ffn2_wg"], p["ffn2_wu"], p["ffn2_wd"], out_rows)


def kernel(x_prompt, x_sample, ffn1_norm, ffn1_wg, ffn1_wu, ffn1_wd, mix_norm, w_in, a_qnorm, a_knorm, a_lq1, a_lk1, a_lq2, a_lk2, a_subln, b_qnorm, b_knorm, c_qa_norm, c_kva_norm, c_wq_up, c_wkv_up, c_qnorm, c_knorm, w_br_a, w_br_b, w_br_c, w_out, ffn2_norm, ffn2_wg, ffn2_wu, ffn2_wd):
    p = dict(ffn1_norm=ffn1_norm, ffn1_wg=ffn1_wg, ffn1_wu=ffn1_wu, ffn1_wd=ffn1_wd, mix_norm=mix_norm, w_in=w_in,
             a_qnorm=a_qnorm, a_knorm=a_knorm, a_lq1=a_lq1, a_lk1=a_lk1, a_lq2=a_lq2, a_lk2=a_lk2, a_subln=a_subln,
             b_qnorm=b_qnorm, b_knorm=b_knorm, c_qa_norm=c_qa_norm, c_kva_norm=c_kva_norm, c_wq_up=c_wq_up,
             c_wkv_up=c_wkv_up, c_qnorm=c_qnorm, c_knorm=c_knorm, w_br_a=w_br_a, w_br_b=w_br_b, w_br_c=w_br_c,
             w_out=w_out, ffn2_norm=ffn2_norm, ffn2_wg=ffn2_wg, ffn2_wu=ffn2_wu, ffn2_wd=ffn2_wd)
    bp, seq, D = x_prompt.shape
    bs, seq_s, _ = x_sample.shape
    assert seq == seq_s
    nb = bp + bs
    x = (x_prompt.reshape(bp * seq, D), x_sample.reshape(bs * seq, D))
    tabs = _rope_tables(seq)
    depth = ffn1_norm.shape[0]
    for l in range(depth):
        x = _layer(x, l, nb, seq, tabs, p, (bp * seq, bs * seq) if l == depth - 1 else None)
    return x[0].reshape(bp, seq, D), x[1].reshape(bs, seq, D)
```

```python
import functools
import math

import jax
import jax.numpy as jnp
from jax import lax
from jax.experimental import pallas as pl
from jax.experimental.pallas import tpu as pltpu

F32 = jnp.float32
BF16 = jnp.bfloat16

RMS_EPS = 1e-6
ROPE_THETA = 10000.0
LOG2E = math.log2(math.e)
LANES = 128
V7X_VMEM_BYTES = 64 * 1024 * 1024
VMEM_LIMIT = V7X_VMEM_BYTES - 8 * 1024 * 1024

A_HEADS, A_DH = 8, 128
A_HW = 2 * A_DH
A_W = A_HEADS * A_HW
B_GROUPS = ((128, 1), (512, 4), (2048, 16))
B_HPG, B_DH = 6, 128
B_HEADS = B_HPG * len(B_GROUPS)
B_W = B_HEADS * B_DH
B_GW = B_HPG * B_DH
C_HEADS, C_Q_RANK, C_KV_RANK, C_NOPE, C_ROPE, C_VDIM = 16, 1024, 512, 128, 64, 128
C_DQK = C_NOPE + C_ROPE
C_HPAD = 2 * LANES
C_OUT = C_HEADS * C_VDIM
C_VT_ROWS = C_VDIM + 16
N_BRANCH = 3
AB_COLS = 3 * A_W + 3 * B_W
C_COLS = C_Q_RANK + C_KV_RANK + C_ROPE
C_COLS_PAD = 1792
NEG_BIG = -1e30
ATT_TK = 1024


def _tile(n, t):
    t = min(n, t)
    assert n % t == 0, (n, t)
    return t


def _params(sem):
    return pltpu.CompilerParams(dimension_semantics=sem, vmem_limit_bytes=VMEM_LIMIT)


def _row(a):
    return a.reshape(1, -1).astype(F32)


def _piece_specs(n, rows_first, tm, width, nj, two_d):
    if n == 1:
        return [pl.BlockSpec((tm, width), (lambda i, j, k: (i, j)) if two_d else (lambda i: (i, 0)))]
    na = rows_first // tm
    if two_d:
        first = lambda i, j, k: (jnp.minimum(i, na - 1), jnp.where(i < na, j, nj - 1))
        second = lambda i, j, k: (jnp.maximum(i - na, 0), jnp.where(i >= na, j, 0))
    else:
        first = lambda i: (jnp.minimum(i, na - 1), 0)
        second = lambda i: (jnp.maximum(i - na, 0), 0)
    return [pl.BlockSpec((tm, width), first), pl.BlockSpec((tm, width), second)]


def _for_piece(i, na, n, fn):
    if n == 1:
        fn(0)
    else:
        pl.when(i < na)(lambda: fn(0))
        pl.when(i >= na)(lambda: fn(1))


def _rmsnorm_kernel(*refs, na):
    x_refs, g_ref, o_ref = refs[:-2], refs[-2], refs[-1]

    def run(p):
        x = x_refs[p][...]
        y = x * lax.rsqrt(jnp.mean(x * x, axis=-1, keepdims=True) + RMS_EPS)
        o_ref[...] = (y * g_ref[...]).astype(o_ref.dtype)

    _for_piece(pl.program_id(0), na, len(x_refs), run)


def _rmsnorm(xs, g, tm=512):
    T, D = sum(x.shape[0] for x in xs), xs[0].shape[1]
    tm = _tile(math.gcd(*[x.shape[0] for x in xs]), tm)
    return pl.pallas_call(
        functools.partial(_rmsnorm_kernel, na=xs[0].shape[0] // tm),
        grid=(T // tm,),
        in_specs=_piece_specs(len(xs), xs[0].shape[0], tm, D, 1, False) + [pl.BlockSpec((1, D), lambda i: (0, 0))],
        out_specs=pl.BlockSpec((tm, D), lambda i: (i, 0)),
        out_shape=jax.ShapeDtypeStruct((T, D), BF16),
        compiler_params=_params(("parallel",)),
        name="rmsnorm",
    )(*xs, _row(g))


def _cast_kernel(x_ref, o_ref, *, rows, cols):
    x = x_ref[...]
    tr, tc = x.shape
    if rows is not None:
        r = pl.program_id(0) * tr + lax.broadcasted_iota(jnp.int32, (tr, tc), 0)
        x = jnp.where(r < rows, x, 0.0)
    if cols is not None:
        c = pl.program_id(1) * tc + lax.broadcasted_iota(jnp.int32, (tr, tc), 1)
        x = jnp.where(c < cols, x, 0.0)
    o_ref[...] = x.astype(o_ref.dtype)


def _cast(w, l, col0=0, ncols=None, rows_out=None, cols_out=None, tr=2048, tc=512):
    _, R, C = w.shape
    ncols = C - col0 if ncols is None else ncols
    rows_out = R if rows_out is None else rows_out
    cols_out = ncols if cols_out is None else cols_out
    tr, tc = min(tr, rows_out), min(tc, cols_out)
    assert col0 % tc == 0
    c0 = col0 // tc
    gr, gc = pl.cdiv(rows_out, tr), pl.cdiv(cols_out, tc)
    rows = R if gr * tr > R else None
    cols = ncols if (gc * tc > ncols and cols_out > ncols) else None
    return pl.pallas_call(
        functools.partial(_cast_kernel, rows=rows, cols=cols),
        grid=(gr, gc),
        in_specs=[pl.BlockSpec((None, tr, tc), lambda i, j: (l, i, j + c0))],
        out_specs=pl.BlockSpec((tr, tc), lambda i, j: (i, j)),
        out_shape=jax.ShapeDtypeStruct((rows_out, cols_out), BF16),
        compiler_params=_params(("parallel", "parallel")),
        name="cast_bf16",
    )(w)


def _cast_t_kernel(x_ref, o_ref, *, rows):
    x = x_ref[0]
    if rows is not None:
        r = pl.program_id(1) * x.shape[0] + lax.broadcasted_iota(jnp.int32, x.shape, 0)
        x = jnp.where(r < rows, x, 0.0)
    o_ref[...] = x.T.astype(o_ref.dtype)


def _cast_t(wt, l, row0, nrows, cols_out=None, tk=1024, tc=512):
    _, N, K = wt.shape
    cols_out = nrows if cols_out is None else cols_out
    tk, tc = min(tk, K), min(tc, cols_out)
    assert row0 % 8 == 0 and K % tk == 0 and cols_out % tc == 0
    return pl.pallas_call(
        functools.partial(_cast_t_kernel, rows=nrows if cols_out > nrows else None),
        grid=(K // tk, cols_out // tc),
        in_specs=[pl.BlockSpec((pl.Element(1), pl.Element(tc), pl.Element(tk)),
                               lambda i, j: (l, pl.multiple_of(row0 + j * tc, 8), pl.multiple_of(i * tk, LANES)))],
        out_specs=pl.BlockSpec((tk, tc), lambda i, j: (i, j)),
        out_shape=jax.ShapeDtypeStruct((K, cols_out), BF16),
        compiler_params=_params(("parallel", "parallel")),
        name="cast_bf16_t",
    )(wt)


def _ffn_up_kernel(h_ref, wg_ref, wu_ref, o_ref):
    h = h_ref[...]
    g = jnp.dot(h, wg_ref[...], preferred_element_type=F32)
    u = jnp.dot(h, wu_ref[...], preferred_element_type=F32)
    o_ref[...] = (g * jax.nn.sigmoid(g) * u).astype(o_ref.dtype)


def _ffn_up(h, wg, wu, tm=1024, tn=512):
    T, D = h.shape
    N = wg.shape[1]
    tm, tn = _tile(T, tm), _tile(N, tn)
    return pl.pallas_call(
        _ffn_up_kernel,
        grid=(T // tm, N // tn),
        in_specs=[pl.BlockSpec((tm, D), lambda i, j: (i, 0)),
                  pl.BlockSpec((D, tn), lambda i, j: (0, j)),
                  pl.BlockSpec((D, tn), lambda i, j: (0, j))],
        out_specs=pl.BlockSpec((tm, tn), lambda i, j: (i, j)),
        out_shape=jax.ShapeDtypeStruct((T, N), BF16),
        compiler_params=_params(("parallel", "parallel")),
        name="ffn_up",
    )(h, wg, wu)


def _mm_res_kernel(*refs, scale, nk, n_x, na):
    a_ref, b_ref = refs[:2]
    x_refs, o_refs = refs[2:2 + n_x], refs[2 + n_x:]
    i, k = pl.program_id(0), pl.program_id(2)

    def part():
        return jnp.dot(a_ref[...], b_ref[...], preferred_element_type=F32)

    def accumulate(p):
        x_ref, o_ref = x_refs[min(p, n_x - 1)], o_refs[min(p, len(o_refs) - 1)]
        if nk == 1:
            o_ref[...] = x_ref[...] + scale * part()
            return

        @pl.when(k == 0)
        def _():
            o_ref[...] = part()

        @pl.when((k > 0) & (k < nk - 1))
        def _():
            o_ref[...] += part()

        @pl.when(k == nk - 1)
        def _():
            o_ref[...] = x_ref[...] + scale * (o_ref[...] + part())

    _for_piece(i, na, max(n_x, len(o_refs)), accumulate)


def _mm_res(a, b, xs, scale, out_rows=None, tm=1024, tn=1024, tk=1024):
    T, K = a.shape
    N = b.shape[1]
    out_rows = (T,) if out_rows is None else out_rows
    tm = _tile(math.gcd(*[x.shape[0] for x in xs], *out_rows), tm)
    tn, tk = _tile(N, tn), _tile(K, tk)
    nk, nj = K // tk, N // tn
    first = xs[0].shape[0] if len(xs) > 1 else out_rows[0]
    out = pl.pallas_call(
        functools.partial(_mm_res_kernel, scale=scale, nk=nk, n_x=len(xs), na=first // tm),
        grid=(T // tm, nj, nk),
        in_specs=[pl.BlockSpec((tm, tk), lambda i, j, k: (i, k)),
                  pl.BlockSpec((tk, tn), lambda i, j, k: (k, j))] + _piece_specs(len(xs), first, tm, tn, nj, True),
        out_specs=_piece_specs(len(out_rows), first, tm, tn, nj, True),
        out_shape=[jax.ShapeDtypeStruct((r, N), F32) for r in out_rows],
        compiler_params=_params(("parallel", "parallel", "arbitrary") if len(out_rows) == 1 else ("arbitrary",) * 3),
        name="mm_residual",
    )(a, b, *xs)
    return tuple(out)


def _rope128(y, cos, sin_signed):
    return y * cos + pltpu.roll(y, LANES // 2, 1) * sin_signed


def _norm_rope128(a, g, cos, sin_signed):
    return _rope128(a * lax.rsqrt(jnp.mean(a * a, axis=-1, keepdims=True) + RMS_EPS) * g, cos, sin_signed)


def _proj_rope_kernel(u_ref, w_ref, g_ref, cos_ref, sin_ref, o_ref, *, row_chunks):
    tm = u_ref.shape[0] // row_chunks
    for r in range(row_chunks):
        rows = slice(r * tm, (r + 1) * tm)
        acc = jnp.dot(u_ref[rows, :], w_ref[...], preferred_element_type=F32)
        for c in range(acc.shape[1] // LANES):
            sl = slice(c * LANES, (c + 1) * LANES)
            o_ref[rows, sl] = _norm_rope128(acc[:, sl], g_ref[:, sl], cos_ref[rows, :],
                                            sin_ref[rows, :]).astype(o_ref.dtype)


def _proj_rope(u, w, gains, cos, sin, seq, tm=1024, tn=256):
    T, D = u.shape
    N = w.shape[1]
    tm, tn = _tile(seq, tm), _tile(N, tn)
    npos = seq // tm
    return pl.pallas_call(
        functools.partial(_proj_rope_kernel, row_chunks=8 if tm % 1024 == 0 else 1),
        grid=(T // tm, N // tn),
        in_specs=[pl.BlockSpec((tm, D), lambda i, j: (i, 0)),
                  pl.BlockSpec((D, tn), lambda i, j: (0, j)),
                  pl.BlockSpec((1, tn), lambda i, j: (0, j)),
                  pl.BlockSpec((tm, LANES), lambda i, j: (i % npos, 0)),
                  pl.BlockSpec((tm, LANES), lambda i, j: (i % npos, 0))],
        out_specs=pl.BlockSpec((tm, tn), lambda i, j: (i, j)),
        out_shape=jax.ShapeDtypeStruct((T, N), BF16),
        compiler_params=_params(("parallel", "parallel")),
        name="proj_rope",
    )(u, w, gains, cos, sin)


def _proj_vt_kernel(u_ref, w_ref, o_ref):
    tm, tk = u_ref.shape[0], o_ref.shape[3]
    tr = math.gcd(256, tk)
    for r0 in range(0, tm, tr):
        acc = jnp.dot(u_ref[r0:r0 + tr, :], w_ref[...], preferred_element_type=F32)
        o_ref[0, r0 // tk, :, r0 % tk:r0 % tk + tr] = acc.T.astype(o_ref.dtype)


def _proj_vt(u, w, heads, tk, tm=1024):
    T, D = u.shape
    hw = w.shape[1] // heads
    tm = _tile(T, tm)
    tk = _tile(tm, tk)
    return pl.pallas_call(
        _proj_vt_kernel,
        grid=(T // tm, heads),
        in_specs=[pl.BlockSpec((tm, D), lambda i, h: (i, 0)), pl.BlockSpec((D, hw), lambda i, h: (0, h))],
        out_specs=pl.BlockSpec((1, tm // tk, hw, tk), lambda i, h: (h, i, 0, 0)),
        out_shape=jax.ShapeDtypeStruct((heads, T // tk, hw, tk), BF16),
        compiler_params=_params(("parallel", "parallel")),
        name="proj_vt",
    )(u, w)


def _proj_b_kernel(u_ref, w_ref, g_ref, cos_ref, sin_ref, o_ref, scr, *, dil):
    t = pl.program_id(1)
    heads = [slice(h * B_DH, (h + 1) * B_DH) for h in range(B_HPG)]
    tm = u_ref.shape[0]
    row_chunks = 8 if tm % 1024 == 0 else 1
    chunks = [slice(r * (tm // row_chunks), (r + 1) * (tm // row_chunks)) for r in range(row_chunks)]

    @pl.when(t < 2)
    def _():
        for rows in chunks:
            acc = jnp.dot(u_ref[rows, :], w_ref[...], preferred_element_type=F32)
            for h, sl in enumerate(heads):
                scr[h, rows, :] = _norm_rope128(acc[:, sl], g_ref[:, sl], cos_ref[rows, :], sin_ref[rows, :])

    @pl.when(t == 2)
    def _():
        for rows in chunks:
            acc = jnp.dot(u_ref[rows, :], w_ref[...], preferred_element_type=F32)
            for h, sl in enumerate(heads):
                scr[h, rows, :] = acc[:, sl]

    n = tm // dil
    for c in range(dil):
        for h, sl in enumerate(heads):
            o_ref[c, :, sl] = scr[h, pl.ds(c, n, stride=dil), :].astype(o_ref.dtype)


def _proj_b(u, w_b, gains, cos, sin, g, nb, seq, tm=1024):
    T, D = u.shape
    dil = B_GROUPS[g][1]
    tm = _tile(seq, tm)
    assert tm % dil == 0 and (tm // dil) % 16 == 0
    npos = seq // tm
    ng = len(B_GROUPS)
    return pl.pallas_call(
        functools.partial(_proj_b_kernel, dil=dil),
        grid=(T // tm, 3),
        in_specs=[pl.BlockSpec((tm, D), lambda i, t: (i, 0)),
                  pl.BlockSpec((D, B_GW), lambda i, t: (0, t * ng + g)),
                  pl.BlockSpec((1, B_GW), lambda i, t: (0, t)),
                  pl.BlockSpec((tm, LANES), lambda i, t: (i % npos, 0)),
                  pl.BlockSpec((tm, LANES), lambda i, t: (i % npos, 0))],
        out_specs=pl.BlockSpec((dil, tm // dil, B_GW), lambda i, t: (i // npos, i % npos, t)),
        out_shape=jax.ShapeDtypeStruct((nb * dil, seq // dil, 3 * B_GW), BF16),
        scratch_shapes=[pltpu.VMEM((B_HPG, tm, B_DH), F32)],
        compiler_params=_params(("parallel", "arbitrary")),
        name=f"proj_b_d{dil}",
    )(u, w_b, gains, cos, sin)


def _mm_kernel(a_ref, b_ref, o_ref):
    o_ref[...] = jnp.dot(a_ref[...], b_ref[...], preferred_element_type=F32).astype(o_ref.dtype)


def _mm(a, b, out_dtype, tm=1024, tn=256):
    T, K = a.shape
    N = b.shape[1]
    tm, tn = _tile(T, tm), _tile(N, tn)
    return pl.pallas_call(
        _mm_kernel,
        grid=(T // tm, N // tn),
        in_specs=[pl.BlockSpec((tm, K), lambda i, j: (i, 0)), pl.BlockSpec((K, tn), lambda i, j: (0, j))],
        out_specs=pl.BlockSpec((tm, tn), lambda i, j: (i, j)),
        out_shape=jax.ShapeDtypeStruct((T, N), out_dtype),
        compiler_params=_params(("parallel", "parallel")),
        name="mm",
    )(a, b)


def _rope64_in128(y, cos, sin_lo, sin_hi):
    q = C_ROPE // 2
    return y * cos + pltpu.roll(y, LANES - q, 1) * sin_lo + pltpu.roll(y, q, 1) * sin_hi


def _mla_q_kernel(cq_ref, ng_ref, w_ref, g_ref, cos_ref, slo_ref, shi_ref, o_ref, a_scr, *, tn):
    @pl.when(pl.program_id(1) == 0)
    def _():
        x = cq_ref[...]
        y = x * lax.rsqrt(jnp.mean(x * x, axis=-1, keepdims=True) + RMS_EPS) * ng_ref[...]
        a_scr[...] = y.astype(a_scr.dtype)

    tm = a_scr.shape[0]
    row_chunks = tm // 256 if tm % 512 == 0 else 1
    for rc in range(row_chunks):
        rows = slice(rc * (tm // row_chunks), (rc + 1) * (tm // row_chunks))
        acc = jnp.dot(a_scr[rows, :], w_ref[...], preferred_element_type=F32)
        for h in range(tn // C_HPAD):
            a = acc[:, h * C_HPAD:(h + 1) * C_HPAD]
            r = lax.rsqrt(jnp.sum(a * a, axis=-1, keepdims=True) * (1.0 / C_DQK) + RMS_EPS)
            y = a * r * g_ref[...]
            o_ref[rows, h * C_HPAD:h * C_HPAD + LANES] = y[:, :LANES].astype(o_ref.dtype)
            o_ref[rows, h * C_HPAD + LANES:(h + 1) * C_HPAD] = _rope64_in128(
                y[:, LANES:], cos_ref[rows, :], slo_ref[rows, :], shi_ref[rows, :]).astype(o_ref.dtype)


def _mla_q(pc, qa_gain, wq_pad, q_gain_pad, cos, slo, shi, seq, tm=1024, tn=1024):
    T = pc.shape[0]
    N = wq_pad.shape[1]
    tm, tn = _tile(seq, tm), _tile(N, tn)
    npos = seq // tm
    pos = lambda i, j: (i % npos, 0)
    return pl.pallas_call(
        functools.partial(_mla_q_kernel, tn=tn),
        grid=(T // tm, N // tn),
        in_specs=[pl.BlockSpec((tm, C_Q_RANK), lambda i, j: (i, 0)),
                  pl.BlockSpec((1, C_Q_RANK), lambda i, j: (0, 0)),
                  pl.BlockSpec((C_Q_RANK, tn), lambda i, j: (0, j)),
                  pl.BlockSpec((1, C_HPAD), lambda i, j: (0, 0)),
                  pl.BlockSpec((tm, LANES), pos), pl.BlockSpec((tm, LANES), pos), pl.BlockSpec((tm, LANES), pos)],
        out_specs=pl.BlockSpec((tm, tn), lambda i, j: (i, j)),
        out_shape=jax.ShapeDtypeStruct((T, N), BF16),
        scratch_shapes=[pltpu.VMEM((tm, C_Q_RANK), BF16)],
        compiler_params=_params(("parallel", "arbitrary")),
        name="mla_q_up",
    )(pc, qa_gain, wq_pad, q_gain_pad, cos, slo, shi)


def _mla_kv_kernel(ckv_ref, ng_ref, kr_ref, w_ref, gn_ref, gr_ref, cos_ref, slo_ref, shi_ref,
                   k_ref, vt_ref, a_scr, *, heads):
    @pl.when(pl.program_id(1) == 0)
    def _():
        x = ckv_ref[...]
        y = x * lax.rsqrt(jnp.mean(x * x, axis=-1, keepdims=True) + RMS_EPS) * ng_ref[...]
        a_scr[...] = y.astype(a_scr.dtype)

    tm, tk = a_scr.shape[0], vt_ref.shape[3]
    tr = math.gcd(256, tk)
    extra = C_VT_ROWS - C_VDIM
    ones_row = jnp.where(lax.broadcasted_iota(jnp.int32, (extra, tr), 0) == 0, 1.0, 0.0).astype(vt_ref.dtype)
    for r0 in range(0, tm, tr):
        rows = slice(r0, r0 + tr)
        c, cols = r0 // tk, slice(r0 % tk, r0 % tk + tr)
        acc = jnp.dot(a_scr[rows, :], w_ref[...], preferred_element_type=F32)
        kr = kr_ref[rows, :]
        kr_ss = jnp.sum(kr * kr, axis=-1, keepdims=True)
        for h in range(heads):
            base = h * (C_NOPE + C_VDIM)
            kn = acc[:, base:base + C_NOPE]
            r = lax.rsqrt((jnp.sum(kn * kn, axis=-1, keepdims=True) + kr_ss) * (1.0 / C_DQK) + RMS_EPS)
            k_ref[rows, h * C_HPAD:h * C_HPAD + LANES] = (kn * r * gn_ref[...]).astype(k_ref.dtype)
            k_ref[rows, h * C_HPAD + LANES:(h + 1) * C_HPAD] = _rope64_in128(
                kr * r * gr_ref[...], cos_ref[rows, :], slo_ref[rows, :], shi_ref[rows, :]).astype(k_ref.dtype)
            vt_ref[h, c, :C_VDIM, cols] = acc[:, base + C_NOPE:base + C_NOPE + C_VDIM].T.astype(vt_ref.dtype)
            vt_ref[h, c, C_VDIM:, cols] = ones_row


def _mla_kv(pc, kva_gain, wkv, k_gain_nope, k_gain_rope, cos, slo, shi, seq, tk, tm=1024, heads=2):
    T = pc.shape[0]
    tm = _tile(seq, tm)
    tk = _tile(tm, tk)
    npos = seq // tm
    pos = lambda i, j: (i % npos, 0)
    hw = C_NOPE + C_VDIM
    return pl.pallas_call(
        functools.partial(_mla_kv_kernel, heads=heads),
        grid=(T // tm, C_HEADS // heads),
        in_specs=[pl.BlockSpec((tm, C_KV_RANK), lambda i, j: (i, C_Q_RANK // C_KV_RANK)),
                  pl.BlockSpec((1, C_KV_RANK), lambda i, j: (0, 0)),
                  pl.BlockSpec((tm, LANES), lambda i, j: (i, (C_Q_RANK + C_KV_RANK) // LANES)),
                  pl.BlockSpec((C_KV_RANK, heads * hw), lambda i, j: (0, j)),
                  pl.BlockSpec((1, LANES), lambda i, j: (0, 0)),
                  pl.BlockSpec((1, LANES), lambda i, j: (0, 0)),
                  pl.BlockSpec((tm, LANES), pos), pl.BlockSpec((tm, LANES), pos), pl.BlockSpec((tm, LANES), pos)],
        out_specs=[pl.BlockSpec((tm, heads * C_HPAD), lambda i, j: (i, j)),
                   pl.BlockSpec((heads, tm // tk, C_VT_ROWS, tk), lambda i, j: (j, i, 0, 0))],
        out_shape=[jax.ShapeDtypeStruct((T, C_HEADS * C_HPAD), BF16),
                   jax.ShapeDtypeStruct((C_HEADS, T // tk, C_VT_ROWS, tk), BF16)],
        scratch_shapes=[pltpu.VMEM((tm, C_KV_RANK), BF16)],
        compiler_params=_params(("parallel", "arbitrary")),
        name="mla_kv_up",
    )(pc, kva_gain, pc, wkv, k_gain_nope, k_gain_rope, cos, slo, shi)


def _nt_dot(a, b):
    return lax.dot_general(a, b, (((1,), (1,)), ((), ())), preferred_element_type=F32)


def _scores_t(k, q, s_ref):
    s = _nt_dot(k, q)
    s_ref[...] = s
    return jnp.max(s, axis=0, keepdims=True)


def _softmax_step_t(s_ref, s_max, vt, m_prev, l_prev, acc_ref, idx):
    m_new = jnp.maximum(m_prev, s_max)
    alpha = jnp.exp2(m_prev - m_new)
    p = jnp.exp2(s_ref[...] - m_new)
    l_new = None if l_prev is None else alpha * l_prev + jnp.sum(p, axis=0, keepdims=True)
    acc_ref[idx] = acc_ref[idx] * alpha + jnp.dot(vt, p.astype(vt.dtype), preferred_element_type=F32)
    return m_new, l_new


def _pipelined_key_loop(nk, scores, consume, init):
    assert nk % 2 == 0

    def pair(jj, carry):
        state, smax0 = carry
        j = 2 * jj
        smax1 = scores(1, j + 1)
        state = consume(0, j, smax0, state)
        smax0 = scores(0, j + 2)
        state = consume(1, j + 1, smax1, state)
        return state, smax0

    state, smax0 = lax.fori_loop(0, nk // 2 - 1, pair, (init, scores(0, 0)))
    smax1 = scores(1, nk - 1)
    state = consume(0, nk - 2, smax0, state)
    return consume(1, nk - 1, smax1, state)


def _diff_attn_kernel(q_ref, k_ref, vt_ref, lq1_ref, lk1_ref, lq2_ref, lk2_ref, g_ref, o_ref, acc_ref, s_ref,
                      *, lam_init):
    tq = q_ref.shape[0]
    tk = vt_ref.shape[3]
    nk = k_ref.shape[0] // tk
    q = q_ref[...]
    acc_ref[...] = jnp.zeros_like(acc_ref)
    subs = [slice(sub * A_DH, (sub + 1) * A_DH) for sub in range(2)]

    def scores(slot, j):
        k = k_ref[pl.ds(pl.multiple_of(j * tk, tk), tk), :]
        return tuple(_scores_t(k[:, sl], q[:, sl], s_ref.at[slot, sub]) for sub, sl in enumerate(subs))

    def consume(slot, j, smax, state):
        vt = vt_ref[0, j]
        m0, l0, m1, l1 = state
        m0, l0 = _softmax_step_t(s_ref.at[slot, 0], smax[0], vt, m0, l0, acc_ref, 0)
        m1, l1 = _softmax_step_t(s_ref.at[slot, 1], smax[1], vt, m1, l1, acc_ref, 1)
        return m0, l0, m1, l1

    neg = jnp.full((1, tq), -jnp.inf, F32)
    zero = jnp.zeros((1, tq), F32)
    _, l0, _, l1 = _pipelined_key_loop(nk, scores, consume, (neg, zero, neg, zero))

    lam = (jnp.exp(jnp.sum(lq1_ref[...] * lk1_ref[...], axis=-1, keepdims=True))
           - jnp.exp(jnp.sum(lq2_ref[...] * lk2_ref[...], axis=-1, keepdims=True)) + lam_init)
    o = (acc_ref[0] * (1.0 / l0) - lam * (acc_ref[1] * (1.0 / l1))).T
    y = o * lax.rsqrt(jnp.mean(o * o, axis=-1, keepdims=True) + RMS_EPS) * g_ref[...]
    o_ref[...] = (y * (1.0 - lam_init)).astype(o_ref.dtype)


def _diff_attn(qk, vt, lq1, lk1, lq2, lk2, subln, nb, seq, lam_init, tq=1024):
    T = qk.shape[0]
    tk = vt.shape[3]
    tq = _tile(seq, tq)
    nq = seq // tq
    small = pl.BlockSpec((1, A_DH), lambda b, h, i: (0, 0))
    return pl.pallas_call(
        functools.partial(_diff_attn_kernel, lam_init=lam_init),
        grid=(nb, A_HEADS, nq),
        in_specs=[pl.BlockSpec((tq, A_HW), lambda b, h, i: (b * nq + i, h)),
                  pl.BlockSpec((seq, A_HW), lambda b, h, i: (b, A_HEADS + h)),
                  pl.BlockSpec((1, seq // tk, A_HW, tk), lambda b, h, i: (h, b, 0, 0)),
                  small, small, small, small,
                  pl.BlockSpec((1, A_HW), lambda b, h, i: (0, 0))],
        out_specs=pl.BlockSpec((tq, A_HW), lambda b, h, i: (b * nq + i, h)),
        out_shape=jax.ShapeDtypeStruct((T, A_W), BF16),
        scratch_shapes=[pltpu.VMEM((2, A_HW, tq), F32), pltpu.VMEM((2, 2, tk, tq), F32)],
        compiler_params=_params(("parallel", "parallel", "parallel")),
        name="diff_attn",
    )(qk, qk, vt, _row(lq1), _row(lk1), _row(lq2), _row(lk2), _row(subln))


def _mla_attn_kernel(q_ref, k_ref, vt_ref, o_ref, acc_ref, s_ref):
    tq = q_ref.shape[0]
    tk = vt_ref.shape[3]
    nk = k_ref.shape[0] // tk
    q = q_ref[...]
    acc_ref[...] = jnp.zeros_like(acc_ref)

    def scores(slot, j):
        return _scores_t(k_ref[pl.ds(pl.multiple_of(j * tk, tk), tk), :], q, s_ref.at[slot])

    def consume(slot, j, smax, m):
        return _softmax_step_t(s_ref.at[slot], smax, vt_ref[0, j], m, None, acc_ref, 0)[0]

    _pipelined_key_loop(nk, scores, consume, jnp.full((1, tq), -jnp.inf, F32))
    l = acc_ref[0, C_VDIM:C_VDIM + 1, :]
    o_ref[...] = (acc_ref[0, :C_VDIM, :] * (1.0 / l)).T.astype(o_ref.dtype)


def _mla_attn(qc, kc, vt, nb, seq, tq=2048):
    T = qc.shape[0]
    tk = vt.shape[3]
    tq = _tile(seq, tq)
    nq = seq // tq
    return pl.pallas_call(
        _mla_attn_kernel,
        grid=(nb, C_HEADS, nq),
        in_specs=[pl.BlockSpec((tq, C_HPAD), lambda b, h, i: (b * nq + i, h)),
                  pl.BlockSpec((seq, C_HPAD), lambda b, h, i: (b, h)),
                  pl.BlockSpec((1, seq // tk, C_VT_ROWS, tk), lambda b, h, i: (h, b, 0, 0))],
        out_specs=pl.BlockSpec((tq, C_VDIM), lambda b, h, i: (b * nq + i, h)),
        out_shape=jax.ShapeDtypeStruct((T, C_OUT), BF16),
        scratch_shapes=[pltpu.VMEM((1, C_VT_ROWS, tq), F32), pltpu.VMEM((2, tk, tq), F32)],
        compiler_params=_params(("parallel", "parallel", "parallel")),
        name="mla_attn",
    )(qc, kc, vt)


def _window_attn_kernel(q_ref, kp_ref, km_ref, kn_ref, vp_ref, vm_ref, vn_ref, o_ref, lse_ref, *, half, sub_len):
    tq = q_ref.shape[0]
    i = pl.program_id(1)
    span = tq + 2 * half
    qpos = i * tq + lax.broadcasted_iota(jnp.int32, (tq, span), 0)
    kpos = i * tq - half + lax.broadcasted_iota(jnp.int32, (tq, span), 1)
    valid = (jnp.abs(kpos - qpos) <= half) & (kpos >= 0) & (kpos < sub_len)
    scale = B_DH ** -0.5
    for h in range(B_HPG):
        sl = slice(h * B_DH, (h + 1) * B_DH)
        k = jnp.concatenate([kp_ref[:, sl], km_ref[:, sl], kn_ref[:, sl]], axis=0)
        v = jnp.concatenate([vp_ref[:, sl], vm_ref[:, sl], vn_ref[:, sl]], axis=0)
        s = jnp.where(valid, _nt_dot(q_ref[:, sl], k) * scale, NEG_BIG)
        m = jnp.max(s, axis=-1, keepdims=True)
        p = jnp.exp(s - m)
        l = jnp.sum(p, axis=-1, keepdims=True)
        o = jnp.dot(p.astype(v.dtype), v, preferred_element_type=F32) / l
        o_ref[:, sl] = o.astype(o_ref.dtype)
        lse_ref[:, sl] = jnp.broadcast_to(m + jnp.log(l), (tq, B_DH))


def _window_attn(qkv, g, tq=256):
    win, dil = B_GROUPS[g]
    half = win // (2 * dil)
    nsub, L, _ = qkv.shape
    tq = _tile(L, tq)
    assert tq % half == 0
    nq = L // tq
    r = tq // half
    nh = L // half
    flat = qkv.reshape(nsub * L, 3 * B_GW)

    def main(col):
        return pl.BlockSpec((tq, B_GW), lambda s, i: (s * nq + i, col))

    def prev(col):
        return pl.BlockSpec((half, B_GW), lambda s, i: (s * nh + jnp.maximum(i * r - 1, 0), col))

    def nxt(col):
        return pl.BlockSpec((half, B_GW), lambda s, i: (s * nh + jnp.minimum((i + 1) * r, nh - 1), col))

    out_spec = pl.BlockSpec((tq, B_GW), lambda s, i: (s * nq + i, 0))
    o, lse = pl.pallas_call(
        functools.partial(_window_attn_kernel, half=half, sub_len=L),
        grid=(nsub, nq),
        in_specs=[main(0), prev(1), main(1), nxt(1), prev(2), main(2), nxt(2)],
        out_specs=[out_spec, out_spec],
        out_shape=[jax.ShapeDtypeStruct((nsub * L, B_GW), BF16), jax.ShapeDtypeStruct((nsub * L, B_GW), F32)],
        compiler_params=_params(("parallel", "parallel")),
        name=f"window_attn_d{dil}",
    )(flat, flat, flat, flat, flat, flat, flat)
    return o.reshape(nsub, L, B_GW), lse.reshape(nsub, L, B_GW)


def _combine_kernel(o0, o1, o2, l0, l1, l2, o_ref, so, sl):
    tm = o_ref.shape[0]
    heads = [slice(h * B_DH, (h + 1) * B_DH) for h in range(B_HPG)]
    for g, (og, lg) in enumerate(((o0, l0), (o1, l1), (o2, l2))):
        dil = og.shape[0]
        n = tm // dil
        for c in range(dil):
            for h, hs in enumerate(heads):
                so[g * B_HPG + h, pl.ds(c, n, stride=dil), :] = og[c, :, hs].astype(F32)
                sl[g * B_HPG + h, pl.ds(c, n, stride=dil), :] = lg[c, :, hs]
    for h, hs in enumerate(heads):
        a0, a1, a2 = sl[h], sl[B_HPG + h], sl[2 * B_HPG + h]
        m = jnp.maximum(jnp.maximum(a0, a1), a2)
        e0, e1, e2 = jnp.exp(a0 - m), jnp.exp(a1 - m), jnp.exp(a2 - m)
        num = e0 * so[h] + e1 * so[B_HPG + h] + e2 * so[2 * B_HPG + h]
        o_ref[:, hs] = (num / (e0 + e1 + e2)).astype(o_ref.dtype)


def _combine(outs, lses, nb, seq, tm=512):
    tm = _tile(seq, tm)
    npos = seq // tm
    ng = len(outs)

    def spec(a):
        dil = a.shape[0] // nb
        return pl.BlockSpec((dil, tm // dil, B_GW), lambda i: (i // npos, i % npos, 0))

    return pl.pallas_call(
        _combine_kernel,
        grid=(nb * npos,),
        in_specs=[spec(a) for a in outs] + [spec(a) for a in lses],
        out_specs=pl.BlockSpec((tm, B_GW), lambda i: (i, 0)),
        out_shape=jax.ShapeDtypeStruct((nb * seq, B_GW), BF16),
        scratch_shapes=[pltpu.VMEM((ng * B_HPG, tm, B_DH), F32), pltpu.VMEM((ng * B_HPG, tm, B_DH), F32)],
        compiler_params=_params(("parallel",)),
        name="window_combine",
    )(*outs, *lses)


def _merge_kernel(u_ref, oa_ref, ob_ref, oc_ref, wga_ref, wgb_ref, wgc_ref, wa_ref, wb_ref, wc_ref, o_ref):
    u = u_ref[...]

    def branch(wg_ref, x_ref, w_ref):
        gate = jax.nn.sigmoid(jnp.dot(u, wg_ref[...], preferred_element_type=F32))
        return gate * jnp.dot(x_ref[...], w_ref[...], preferred_element_type=F32)

    o_ref[...] = (branch(wga_ref, oa_ref, wa_ref) + branch(wgb_ref, ob_ref, wb_ref)
                  + branch(wgc_ref, oc_ref, wc_ref)).astype(o_ref.dtype)


def _merge(u, oa, ob, oc, w_gates, wa, wb, wc, tm=512, tn=256):
    T, D = u.shape
    tm, tn = _tile(T, tm), _tile(D, tn)
    nj = D // tn
    row = lambda w: pl.BlockSpec((tm, w), lambda i, j: (i, 0))
    col = lambda k: pl.BlockSpec((k, tn), lambda i, j: (0, j))
    gate = lambda n: pl.BlockSpec((D, tn), lambda i, j: (0, n * nj + j))
    return pl.pallas_call(
        _merge_kernel,
        grid=(T // tm, nj),
        in_specs=[row(D), row(oa.shape[1]), row(ob.shape[1]), row(oc.shape[1]),
                  gate(0), gate(1), gate(2), col(wa.shape[0]), col(wb.shape[0]), col(wc.shape[0])],
        out_specs=pl.BlockSpec((tm, tn), lambda i, j: (i, j)),
        out_shape=jax.ShapeDtypeStruct((T, D), BF16),
        compiler_params=_params(("parallel", "parallel")),
        name="gated_merge",
    )(u, oa, ob, oc, w_gates, w_gates, w_gates, wa, wb, wc)


def _rope_tables(seq):
    pos = jnp.arange(seq, dtype=F32)[:, None]

    def ang(d):
        half = d // 2
        inv = 1.0 / (ROPE_THETA ** (jnp.arange(half, dtype=F32) * (2.0 / d)))
        return pos * inv[None, :]

    a = ang(A_DH)
    cos128 = jnp.concatenate([jnp.cos(a), jnp.cos(a)], axis=-1)
    sin128 = jnp.concatenate([-jnp.sin(a), jnp.sin(a)], axis=-1)
    c = ang(C_ROPE)
    z = jnp.zeros_like(c)
    cos64 = jnp.concatenate([jnp.cos(c), jnp.cos(c), z, z], axis=-1)
    sin_lo = jnp.concatenate([-jnp.sin(c), z, z, z], axis=-1)
    sin_hi = jnp.concatenate([z, jnp.sin(c), z, z], axis=-1)
    return cos128, sin128, cos64, sin_lo, sin_hi


def _layer(x, l, nb, seq, tabs, p, out_rows):
    cos128, sin128, cos64, sin_lo, sin_hi = tabs
    D = x[0].shape[1]
    ff = p["ffn1_wg"].shape[2]
    ffp = -(-ff // 1024) * 1024

    def ffn(x, norm, wg, wu, wd, out_rows=None):
        h = _rmsnorm(x, norm[l])
        act = _ffn_up(h, _cast(wg, l, cols_out=ffp), _cast(wu, l, cols_out=ffp))
        return _mm_res(act, _cast(wd, l, rows_out=ffp), x, 0.5, out_rows, tk=ffp // 4)

    x = ffn(x, p["ffn1_norm"], p["ffn1_wg"], p["ffn1_wu"], p["ffn1_wd"])

    u = _rmsnorm(x, p["mix_norm"][l])
    w_in_t = jnp.swapaxes(p["w_in"], 1, 2)
    b0 = 3 * A_W
    tile_h = lambda g, n: jnp.tile(g, n)

    a_gains = jnp.concatenate([tile_h(p["a_qnorm"][l] * (A_DH ** -0.5 * LOG2E), A_W // A_DH),
                               tile_h(p["a_knorm"][l], A_W // A_DH)]).reshape(1, 2 * A_W)
    a_qk = _proj_rope(u, _cast_t(w_in_t, l, 0, 2 * A_W), a_gains, cos128, sin128, seq)
    a_vt = _proj_vt(u, _cast_t(w_in_t, l, 2 * A_W, A_W), A_HEADS, ATT_TK)
    lam_init = 0.8 - 0.6 * math.exp(-0.3 * l)
    oa = _diff_attn(a_qk, a_vt, p["a_lq1"][l], p["a_lk1"][l], p["a_lq2"][l], p["a_lk2"][l], p["a_subln"][l],
                    nb, seq, lam_init)

    b_gains = jnp.concatenate([tile_h(p["b_qnorm"][l], B_HPG), tile_h(p["b_knorm"][l], B_HPG),
                               jnp.ones((B_GW,), F32)]).reshape(1, 3 * B_GW)
    w_b = _cast_t(w_in_t, l, b0, 3 * B_W, tc=256)
    wres = [_window_attn(_proj_b(u, w_b, b_gains, cos128, sin128, g, nb, seq), g) for g in range(len(B_GROUPS))]
    ob = _combine([r[0] for r in wres], [r[1] for r in wres], nb, seq)

    pc = _mm(u, _cast_t(w_in_t, l, AB_COLS, C_COLS, cols_out=C_COLS_PAD, tc=256), F32)
    wq = p["c_wq_up"][l].reshape(C_Q_RANK, C_HEADS, C_DQK)
    wq_pad = jnp.pad(wq, ((0, 0), (0, 0), (0, C_HPAD - C_DQK))).reshape(C_Q_RANK, C_HEADS * C_HPAD).astype(BF16)
    qg = p["c_qnorm"][l] * (C_DQK ** -0.5 * LOG2E)
    kg = p["c_knorm"][l]
    pad_to = lambda v, n: jnp.pad(v, (0, n - v.shape[0]))
    qc = _mla_q(pc, _row(p["c_qa_norm"][l]), wq_pad, _row(pad_to(qg, C_HPAD)), cos64, sin_lo, sin_hi, seq)
    kc, c_vt = _mla_kv(pc, _row(p["c_kva_norm"][l]), p["c_wkv_up"][l].astype(BF16), _row(kg[:C_NOPE]),
                       _row(pad_to(kg[C_NOPE:], LANES)), cos64, sin_lo, sin_hi, seq, ATT_TK)
    oc = _mla_attn(qc, kc, c_vt, nb, seq)

    m = _merge(u, oa, ob, oc, _cast_t(w_in_t, l, AB_COLS + C_COLS, N_BRANCH * D), _cast(p["w_br_a"], l),
               _cast(p["w_br_b"], l), _cast(p["w_br_c"], l))
    x = _mm_res(m, _cast(p["w_out"], l), x, 1.0, tm=1024, tn=512, tk=D)

    return ffn(x, p["ffn2_norm"], p["ffn2_wg"], p["ffn2_wu"], p["ffn2_wd"], out_rows)


def kernel(x_prompt, x_sample, ffn1_norm, ffn1_wg, ffn1_wu, ffn1_wd, mix_norm, w_in, a_qnorm, a_knorm, a_lq1, a_lk1, a_lq2, a_lk2, a_subln, b_qnorm, b_knorm, c_qa_norm, c_kva_norm, c_wq_up, c_wkv_up, c_qnorm, c_knorm, w_br_a, w_br_b, w_br_c, w_out, ffn2_norm, ffn2_wg, ffn2_wu, ffn2_wd):
    p = dict(ffn1_norm=ffn1_norm, ffn1_wg=ffn1_wg, ffn1_wu=ffn1_wu, ffn1_wd=ffn1_wd, mix_norm=mix_norm, w_in=w_in,
             a_qnorm=a_qnorm, a_knorm=a_knorm, a_lq1=a_lq1, a_lk1=a_lk1, a_lq2=a_lq2, a_lk2=a_lk2, a_subln=a_subln,
             b_qnorm=b_qnorm, b_knorm=b_knorm, c_qa_norm=c_qa_norm, c_kva_norm=c_kva_norm, c_wq_up=c_wq_up,
             c_wkv_up=c_wkv_up, c_qnorm=c_qnorm, c_knorm=c_knorm, w_br_a=w_br_a, w_br_b=w_br_b, w_br_c=w_br_c,
             w_out=w_out, ffn2_norm=ffn2_norm, ffn2_wg=ffn2_wg, ffn2_wu=ffn2_wu, ffn2_wd=ffn2_wd)
    bp, seq, D = x_prompt.shape
    bs, seq_s, _ = x_sample.shape
    assert seq == seq_s
    nb = bp + bs
    x = (x_prompt.reshape(bp * seq, D), x_sample.reshape(bs * seq, D))
    tabs = _rope_tables(seq)
    depth = ffn1_norm.shape[0]
    for l in range(depth):
        x = _layer(x, l, nb, seq, tabs, p, (bp * seq, bs * seq) if l == depth - 1 else None)
    return x[0].reshape(bp, seq, D), x[1].reshape(bs, seq, D)
```

```python
import functools
import math

import jax
import jax.numpy as jnp
from jax import lax
from jax.experimental import pallas as pl
from jax.experimental.pallas import tpu as pltpu

F32 = jnp.float32
BF16 = jnp.bfloat16

RMS_EPS = 1e-6
ROPE_THETA = 10000.0
LOG2E = math.log2(math.e)
LANES = 128
V7X_VMEM_BYTES = 64 * 1024 * 1024
VMEM_LIMIT = V7X_VMEM_BYTES - 8 * 1024 * 1024

A_HEADS, A_DH = 8, 128
A_HW = 2 * A_DH
A_W = A_HEADS * A_HW
B_GROUPS = ((128, 1), (512, 4), (2048, 16))
B_HPG, B_DH = 6, 128
B_HEADS = B_HPG * len(B_GROUPS)
B_W = B_HEADS * B_DH
B_GW = B_HPG * B_DH
C_HEADS, C_Q_RANK, C_KV_RANK, C_NOPE, C_ROPE, C_VDIM = 16, 1024, 512, 128, 64, 128
C_DQK = C_NOPE + C_ROPE
C_HPAD = 2 * LANES
C_OUT = C_HEADS * C_VDIM
C_VT_ROWS = C_VDIM + 16
N_BRANCH = 3
AB_COLS = 3 * A_W + 3 * B_W
C_COLS = C_Q_RANK + C_KV_RANK + C_ROPE
C_COLS_PAD = 1792
NEG_BIG = -1e30
ATT_TK = 1024


def _tile(n, t):
    t = min(n, t)
    assert n % t == 0, (n, t)
    return t


def _params(sem):
    return pltpu.CompilerParams(dimension_semantics=sem, vmem_limit_bytes=VMEM_LIMIT)


def _row(a):
    return a.reshape(1, -1).astype(F32)


def _piece_specs(n, rows_first, tm, width, nj, two_d):
    if n == 1:
        return [pl.BlockSpec((tm, width), (lambda i, j, k: (i, j)) if two_d else (lambda i: (i, 0)))]
    na = rows_first // tm
    if two_d:
        first = lambda i, j, k: (jnp.minimum(i, na - 1), jnp.where(i < na, j, nj - 1))
        second = lambda i, j, k: (jnp.maximum(i - na, 0), jnp.where(i >= na, j, 0))
    else:
        first = lambda i: (jnp.minimum(i, na - 1), 0)
        second = lambda i: (jnp.maximum(i - na, 0), 0)
    return [pl.BlockSpec((tm, width), first), pl.BlockSpec((tm, width), second)]


def _for_piece(i, na, n, fn):
    if n == 1:
        fn(0)
    else:
        pl.when(i < na)(lambda: fn(0))
        pl.when(i >= na)(lambda: fn(1))


def _rmsnorm_kernel(*refs, na):
    x_refs, g_ref, o_ref = refs[:-2], refs[-2], refs[-1]

    def run(p):
        x = x_refs[p][...]
        y = x * lax.rsqrt(jnp.mean(x * x, axis=-1, keepdims=True) + RMS_EPS)
        o_ref[...] = (y * g_ref[...]).astype(o_ref.dtype)

    _for_piece(pl.program_id(0), na, len(x_refs), run)


def _rmsnorm(xs, g, tm=512):
    T, D = sum(x.shape[0] for x in xs), xs[0].shape[1]
    tm = _tile(math.gcd(*[x.shape[0] for x in xs]), tm)
    return pl.pallas_call(
        functools.partial(_rmsnorm_kernel, na=xs[0].shape[0] // tm),
        grid=(T // tm,),
        in_specs=_piece_specs(len(xs), xs[0].shape[0], tm, D, 1, False) + [pl.BlockSpec((1, D), lambda i: (0, 0))],
        out_specs=pl.BlockSpec((tm, D), lambda i: (i, 0)),
        out_shape=jax.ShapeDtypeStruct((T, D), BF16),
        compiler_params=_params(("parallel",)),
        name="rmsnorm",
    )(*xs, _row(g))


def _cast_kernel(x_ref, o_ref, *, rows, cols):
    x = x_ref[...]
    tr, tc = x.shape
    if rows is not None:
        r = pl.program_id(0) * tr + lax.broadcasted_iota(jnp.int32, (tr, tc), 0)
        x = jnp.where(r < rows, x, 0.0)
    if cols is not None:
        c = pl.program_id(1) * tc + lax.broadcasted_iota(jnp.int32, (tr, tc), 1)
        x = jnp.where(c < cols, x, 0.0)
    o_ref[...] = x.astype(o_ref.dtype)


def _cast(w, l, col0=0, ncols=None, rows_out=None, cols_out=None, tr=2048, tc=512):
    _, R, C = w.shape
    ncols = C - col0 if ncols is None else ncols
    rows_out = R if rows_out is None else rows_out
    cols_out = ncols if cols_out is None else cols_out
    tr, tc = min(tr, rows_out), min(tc, cols_out)
    assert col0 % tc == 0
    c0 = col0 // tc
    gr, gc = pl.cdiv(rows_out, tr), pl.cdiv(cols_out, tc)
    rows = R if gr * tr > R else None
    cols = ncols if (gc * tc > ncols and cols_out > ncols) else None
    return pl.pallas_call(
        functools.partial(_cast_kernel, rows=rows, cols=cols),
        grid=(gr, gc),
        in_specs=[pl.BlockSpec((None, tr, tc), lambda i, j: (l, i, j + c0))],
        out_specs=pl.BlockSpec((tr, tc), lambda i, j: (i, j)),
        out_shape=jax.ShapeDtypeStruct((rows_out, cols_out), BF16),
        compiler_params=_params(("parallel", "parallel")),
        name="cast_bf16",
    )(w)


def _cast_t_kernel(x_ref, o_ref, *, rows):
    x = x_ref[0]
    if rows is not None:
        r = pl.program_id(1) * x.shape[0] + lax.broadcasted_iota(jnp.int32, x.shape, 0)
        x = jnp.where(r < rows, x, 0.0)
    o_ref[...] = x.T.astype(o_ref.dtype)


def _cast_t(wt, l, row0, nrows, cols_out=None, tk=1024, tc=512):
    _, N, K = wt.shape
    cols_out = nrows if cols_out is None else cols_out
    tk, tc = min(tk, K), min(tc, cols_out)
    assert row0 % 8 == 0 and K % tk == 0 and cols_out % tc == 0
    return pl.pallas_call(
        functools.partial(_cast_t_kernel, rows=nrows if cols_out > nrows else None),
        grid=(K // tk, cols_out // tc),
        in_specs=[pl.BlockSpec((pl.Element(1), pl.Element(tc), pl.Element(tk)),
                               lambda i, j: (l, pl.multiple_of(row0 + j * tc, 8), pl.multiple_of(i * tk, LANES)))],
        out_specs=pl.BlockSpec((tk, tc), lambda i, j: (i, j)),
        out_shape=jax.ShapeDtypeStruct((K, cols_out), BF16),
        compiler_params=_params(("parallel", "parallel")),
        name="cast_bf16_t",
    )(wt)


def _ffn_up_kernel(h_ref, wg_ref, wu_ref, o_ref):
    h = h_ref[...]
    g = jnp.dot(h, wg_ref[...], preferred_element_type=F32)
    u = jnp.dot(h, wu_ref[...], preferred_element_type=F32)
    o_ref[...] = (g * jax.nn.sigmoid(g) * u).astype(o_ref.dtype)


def _ffn_up(h, wg, wu, tm=1024, tn=512):
    T, D = h.shape
    N = wg.shape[1]
    tm, tn = _tile(T, tm), _tile(N, tn)
    return pl.pallas_call(
        _ffn_up_kernel,
        grid=(T // tm, N // tn),
        in_specs=[pl.BlockSpec((tm, D), lambda i, j: (i, 0)),
                  pl.BlockSpec((D, tn), lambda i, j: (0, j)),
                  pl.BlockSpec((D, tn), lambda i, j: (0, j))],
        out_specs=pl.BlockSpec((tm, tn), lambda i, j: (i, j)),
        out_shape=jax.ShapeDtypeStruct((T, N), BF16),
        compiler_params=_params(("parallel", "parallel")),
        name="ffn_up",
    )(h, wg, wu)


def _mm_res_kernel(*refs, scale, nk, n_x, na):
    a_ref, b_ref = refs[:2]
    x_refs, o_refs = refs[2:2 + n_x], refs[2 + n_x:]
    i, k = pl.program_id(0), pl.program_id(2)

    def part():
        return jnp.dot(a_ref[...], b_ref[...], preferred_element_type=F32)

    def accumulate(p):
        x_ref, o_ref = x_refs[min(p, n_x - 1)], o_refs[min(p, len(o_refs) - 1)]
        if nk == 1:
            o_ref[...] = x_ref[...] + scale * part()
            return

        @pl.when(k == 0)
        def _():
            o_ref[...] = part()

        @pl.when((k > 0) & (k < nk - 1))
        def _():
            o_ref[...] += part()

        @pl.when(k == nk - 1)
        def _():
            o_ref[...] = x_ref[...] + scale * (o_ref[...] + part())

    _for_piece(i, na, max(n_x, len(o_refs)), accumulate)


def _mm_res(a, b, xs, scale, out_rows=None, tm=1024, tn=1024, tk=1024):
    T, K = a.shape
    N = b.shape[1]
    out_rows = (T,) if out_rows is None else out_rows
    tm = _tile(math.gcd(*[x.shape[0] for x in xs], *out_rows), tm)
    tn, tk = _tile(N, tn), _tile(K, tk)
    nk, nj = K // tk, N // tn
    first = xs[0].shape[0] if len(xs) > 1 else out_rows[0]
    out = pl.pallas_call(
        functools.partial(_mm_res_kernel, scale=scale, nk=nk, n_x=len(xs), na=first // tm),
        grid=(T // tm, nj, nk),
        in_specs=[pl.BlockSpec((tm, tk), lambda i, j, k: (i, k)),
                  pl.BlockSpec((tk, tn), lambda i, j, k: (k, j))] + _piece_specs(len(xs), first, tm, tn, nj, True),
        out_specs=_piece_specs(len(out_rows), first, tm, tn, nj, True),
        out_shape=[jax.ShapeDtypeStruct((r, N), F32) for r in out_rows],
        compiler_params=_params(("parallel", "parallel", "arbitrary") if len(out_rows) == 1 else ("arbitrary",) * 3),
        name="mm_residual",
    )(a, b, *xs)
    return tuple(out)


def _rope128(y, cos, sin_signed):
    return y * cos + pltpu.roll(y, LANES // 2, 1) * sin_signed


def _norm_rope128(a, g, cos, sin_signed):
    return _rope128(a * lax.rsqrt(jnp.mean(a * a, axis=-1, keepdims=True) + RMS_EPS) * g, cos, sin_signed)


def _proj_rope_kernel(u_ref, w_ref, g_ref, cos_ref, sin_ref, o_ref, *, row_chunks):
    tm = u_ref.shape[0] // row_chunks
    for r in range(row_chunks):
        rows = slice(r * tm, (r + 1) * tm)
        acc = jnp.dot(u_ref[rows, :], w_ref[...], preferred_element_type=F32)
        for c in range(acc.shape[1] // LANES):
            sl = slice(c * LANES, (c + 1) * LANES)
            o_ref[rows, sl] = _norm_rope128(acc[:, sl], g_ref[:, sl], cos_ref[rows, :],
                                            sin_ref[rows, :]).astype(o_ref.dtype)


def _proj_rope(u, w, gains, cos, sin, seq, tm=1024, tn=256):
    T, D = u.shape
    N = w.shape[1]
    tm, tn = _tile(seq, tm), _tile(N, tn)
    npos = seq // tm
    return pl.pallas_call(
        functools.partial(_proj_rope_kernel, row_chunks=8 if tm % 1024 == 0 else 1),
        grid=(T // tm, N // tn),
        in_specs=[pl.BlockSpec((tm, D), lambda i, j: (i, 0)),
                  pl.BlockSpec((D, tn), lambda i, j: (0, j)),
                  pl.BlockSpec((1, tn), lambda i, j: (0, j)),
                  pl.BlockSpec((tm, LANES), lambda i, j: (i % npos, 0)),
                  pl.BlockSpec((tm, LANES), lambda i, j: (i % npos, 0))],
        out_specs=pl.BlockSpec((tm, tn), lambda i, j: (i, j)),
        out_shape=jax.ShapeDtypeStruct((T, N), BF16),
        compiler_params=_params(("parallel", "parallel")),
        name="proj_rope",
    )(u, w, gains, cos, sin)


def _proj_vt_kernel(u_ref, w_ref, o_ref):
    tm, tk = u_ref.shape[0], o_ref.shape[3]
    tr = math.gcd(256, tk)
    for r0 in range(0, tm, tr):
        acc = jnp.dot(u_ref[r0:r0 + tr, :], w_ref[...], preferred_element_type=F32)
        o_ref[0, r0 // tk, :, r0 % tk:r0 % tk + tr] = acc.T.astype(o_ref.dtype)


def _proj_vt(u, w, heads, tk, tm=1024):
    T, D = u.shape
    hw = w.shape[1] // heads
    tm = _tile(T, tm)
    tk = _tile(tm, tk)
    return pl.pallas_call(
        _proj_vt_kernel,
        grid=(T // tm, heads),
        in_specs=[pl.BlockSpec((tm, D), lambda i, h: (i, 0)), pl.BlockSpec((D, hw), lambda i, h: (0, h))],
        out_specs=pl.BlockSpec((1, tm // tk, hw, tk), lambda i, h: (h, i, 0, 0)),
        out_shape=jax.ShapeDtypeStruct((heads, T // tk, hw, tk), BF16),
        compiler_params=_params(("parallel", "parallel")),
        name="proj_vt",
    )(u, w)


def _proj_b_kernel(u_ref, w_ref, g_ref, cos_ref, sin_ref, o_ref, scr, *, dil):
    t = pl.program_id(1)
    heads = [slice(h * B_DH, (h + 1) * B_DH) for h in range(B_HPG)]
    tm = u_ref.shape[0]
    row_chunks = 8 if tm % 1024 == 0 else 1
    chunks = [slice(r * (tm // row_chunks), (r + 1) * (tm // row_chunks)) for r in range(row_chunks)]

    @pl.when(t < 2)
    def _():
        for rows in chunks:
            acc = jnp.dot(u_ref[rows, :], w_ref[...], preferred_element_type=F32)
            for h, sl in enumerate(heads):
                scr[h, rows, :] = _norm_rope128(acc[:, sl], g_ref[:, sl], cos_ref[rows, :], sin_ref[rows, :])

    @pl.when(t == 2)
    def _():
        for rows in chunks:
            acc = jnp.dot(u_ref[rows, :], w_ref[...], preferred_element_type=F32)
            for h, sl in enumerate(heads):
                scr[h, rows, :] = acc[:, sl]

    n = tm // dil
    for c in range(dil):
        for h, sl in enumerate(heads):
            o_ref[c, :, sl] = scr[h, pl.ds(c, n, stride=dil), :].astype(o_ref.dtype)


def _proj_b(u, w_b, gains, cos, sin, g, nb, seq, tm=1024):
    T, D = u.shape
    dil = B_GROUPS[g][1]
    tm = _tile(seq, tm)
    assert tm % dil == 0 and (tm // dil) % 16 == 0
    npos = seq // tm
    ng = len(B_GROUPS)
    return pl.pallas_call(
        functools.partial(_proj_b_kernel, dil=dil),
        grid=(T // tm, 3),
        in_specs=[pl.BlockSpec((tm, D), lambda i, t: (i, 0)),
                  pl.BlockSpec((D, B_GW), lambda i, t: (0, t * ng + g)),
                  pl.BlockSpec((1, B_GW), lambda i, t: (0, t)),
                  pl.BlockSpec((tm, LANES), lambda i, t: (i % npos, 0)),
                  pl.BlockSpec((tm, LANES), lambda i, t: (i % npos, 0))],
        out_specs=pl.BlockSpec((dil, tm // dil, B_GW), lambda i, t: (i // npos, i % npos, t)),
        out_shape=jax.ShapeDtypeStruct((nb * dil, seq // dil, 3 * B_GW), BF16),
        scratch_shapes=[pltpu.VMEM((B_HPG, tm, B_DH), F32)],
        compiler_params=_params(("parallel", "arbitrary")),
        name=f"proj_b_d{dil}",
    )(u, w_b, gains, cos, sin)


def _mm_kernel(a_ref, b_ref, o_ref):
    o_ref[...] = jnp.dot(a_ref[...], b_ref[...], preferred_element_type=F32).astype(o_ref.dtype)


def _mm(a, b, out_dtype, tm=1024, tn=256):
    T, K = a.shape
    N = b.shape[1]
    tm, tn = _tile(T, tm), _tile(N, tn)
    return pl.pallas_call(
        _mm_kernel,
        grid=(T // tm, N // tn),
        in_specs=[pl.BlockSpec((tm, K), lambda i, j: (i, 0)), pl.BlockSpec((K, tn), lambda i, j: (0, j))],
        out_specs=pl.BlockSpec((tm, tn), lambda i, j: (i, j)),
        out_shape=jax.ShapeDtypeStruct((T, N), out_dtype),
        compiler_params=_params(("parallel", "parallel")),
        name="mm",
    )(a, b)


def _rope64_in128(y, cos, sin_lo, sin_hi):
    q = C_ROPE // 2
    return y * cos + pltpu.roll(y, LANES - q, 1) * sin_lo + pltpu.roll(y, q, 1) * sin_hi


def _mla_q_kernel(cq_ref, ng_ref, w_ref, g_ref, cos_ref, slo_ref, shi_ref, o_ref, a_scr, *, tn):
    @pl.when(pl.program_id(1) == 0)
    def _():
        x = cq_ref[...]
        y = x * lax.rsqrt(jnp.mean(x * x, axis=-1, keepdims=True) + RMS_EPS) * ng_ref[...]
        a_scr[...] = y.astype(a_scr.dtype)

    tm = a_scr.shape[0]
    row_chunks = tm // 256 if tm % 512 == 0 else 1
    for rc in range(row_chunks):
        rows = slice(rc * (tm // row_chunks), (rc + 1) * (tm // row_chunks))
        acc = jnp.dot(a_scr[rows, :], w_ref[...], preferred_element_type=F32)
        for h in range(tn // C_HPAD):
            a = acc[:, h * C_HPAD:(h + 1) * C_HPAD]
            r = lax.rsqrt(jnp.sum(a * a, axis=-1, keepdims=True) * (1.0 / C_DQK) + RMS_EPS)
            y = a * r * g_ref[...]
            o_ref[rows, h * C_HPAD:h * C_HPAD + LANES] = y[:, :LANES].astype(o_ref.dtype)
            o_ref[rows, h * C_HPAD + LANES:(h + 1) * C_HPAD] = _rope64_in128(
                y[:, LANES:], cos_ref[rows, :], slo_ref[rows, :], shi_ref[rows, :]).astype(o_ref.dtype)


def _mla_q(pc, qa_gain, wq_pad, q_gain_pad, cos, slo, shi, seq, tm=1024, tn=1024):
    T = pc.shape[0]
    N = wq_pad.shape[1]
    tm, tn = _tile(seq, tm), _tile(N, tn)
    npos = seq // tm
    pos = lambda i, j: (i % npos, 0)
    return pl.pallas_call(
        functools.partial(_mla_q_kernel, tn=tn),
        grid=(T // tm, N // tn),
        in_specs=[pl.BlockSpec((tm, C_Q_RANK), lambda i, j: (i, 0)),
                  pl.BlockSpec((1, C_Q_RANK), lambda i, j: (0, 0)),
                  pl.BlockSpec((C_Q_RANK, tn), lambda i, j: (0, j)),
                  pl.BlockSpec((1, C_HPAD), lambda i, j: (0, 0)),
                  pl.BlockSpec((tm, LANES), pos), pl.BlockSpec((tm, LANES), pos), pl.BlockSpec((tm, LANES), pos)],
        out_specs=pl.BlockSpec((tm, tn), lambda i, j: (i, j)),
        out_shape=jax.ShapeDtypeStruct((T, N), BF16),
        scratch_shapes=[pltpu.VMEM((tm, C_Q_RANK), BF16)],
        compiler_params=_params(("parallel", "arbitrary")),
        name="mla_q_up",
    )(pc, qa_gain, wq_pad, q_gain_pad, cos, slo, shi)


def _mla_kv_kernel(ckv_ref, ng_ref, kr_ref, w_ref, gn_ref, gr_ref, cos_ref, slo_ref, shi_ref,
                   k_ref, vt_ref, a_scr, *, heads):
    @pl.when(pl.program_id(1) == 0)
    def _():
        x = ckv_ref[...]
        y = x * lax.rsqrt(jnp.mean(x * x, axis=-1, keepdims=True) + RMS_EPS) * ng_ref[...]
        a_scr[...] = y.astype(a_scr.dtype)

    tm, tk = a_scr.shape[0], vt_ref.shape[3]
    tr = math.gcd(256, tk)
    extra = C_VT_ROWS - C_VDIM
    ones_row = jnp.where(lax.broadcasted_iota(jnp.int32, (extra, tr), 0) == 0, 1.0, 0.0).astype(vt_ref.dtype)
    for r0 in range(0, tm, tr):
        rows = slice(r0, r0 + tr)
        c, cols = r0 // tk, slice(r0 % tk, r0 % tk + tr)
        acc = jnp.dot(a_scr[rows, :], w_ref[...], preferred_element_type=F32)
        kr = kr_ref[rows, :]
        kr_ss = jnp.sum(kr * kr, axis=-1, keepdims=True)
        for h in range(heads):
            base = h * (C_NOPE + C_VDIM)
            kn = acc[:, base:base + C_NOPE]
            r = lax.rsqrt((jnp.sum(kn * kn, axis=-1, keepdims=True) + kr_ss) * (1.0 / C_DQK) + RMS_EPS)
            k_ref[rows, h * C_HPAD:h * C_HPAD + LANES] = (kn * r * gn_ref[...]).astype(k_ref.dtype)
            k_ref[rows, h * C_HPAD + LANES:(h + 1) * C_HPAD] = _rope64_in128(
                kr * r * gr_ref[...], cos_ref[rows, :], slo_ref[rows, :], shi_ref[rows, :]).astype(k_ref.dtype)
            vt_ref[h, c, :C_VDIM, cols] = acc[:, base + C_NOPE:base + C_NOPE + C_VDIM].T.astype(vt_ref.dtype)
            vt_ref[h, c, C_VDIM:, cols] = ones_row


def _mla_kv(pc, kva_gain, wkv, k_gain_nope, k_gain_rope, cos, slo, shi, seq, tk, tm=1024, heads=2):
    T = pc.shape[0]
    tm = _tile(seq, tm)
    tk = _tile(tm, tk)
    npos = seq // tm
    pos = lambda i, j: (i % npos, 0)
    hw = C_NOPE + C_VDIM
    return pl.pallas_call(
        functools.partial(_mla_kv_kernel, heads=heads),
        grid=(T // tm, C_HEADS // heads),
        in_specs=[pl.BlockSpec((tm, C_KV_RANK), lambda i, j: (i, C_Q_RANK // C_KV_RANK)),
                  pl.BlockSpec((1, C_KV_RANK), lambda i, j: (0, 0)),
                  pl.BlockSpec((tm, LANES), lambda i, j: (i, (C_Q_RANK + C_KV_RANK) // LANES)),
                  pl.BlockSpec((C_KV_RANK, heads * hw), lambda i, j: (0, j)),
                  pl.BlockSpec((1, LANES), lambda i, j: (0, 0)),
                  pl.BlockSpec((1, LANES), lambda i, j: (0, 0)),
                  pl.BlockSpec((tm, LANES), pos), pl.BlockSpec((tm, LANES), pos), pl.BlockSpec((tm, LANES), pos)],
        out_specs=[pl.BlockSpec((tm, heads * C_HPAD), lambda i, j: (i, j)),
                   pl.BlockSpec((heads, tm // tk, C_VT_ROWS, tk), lambda i, j: (j, i, 0, 0))],
        out_shape=[jax.ShapeDtypeStruct((T, C_HEADS * C_HPAD), BF16),
                   jax.ShapeDtypeStruct((C_HEADS, T // tk, C_VT_ROWS, tk), BF16)],
        scratch_shapes=[pltpu.VMEM((tm, C_KV_RANK), BF16)],
        compiler_params=_params(("parallel", "arbitrary")),
        name="mla_kv_up",
    )(pc, kva_gain, pc, wkv, k_gain_nope, k_gain_rope, cos, slo, shi)


def _nt_dot(a, b):
    return lax.dot_general(a, b, (((1,), (1,)), ((), ())), preferred_element_type=F32)


def _scores_t(k, q, s_ref):
    s = _nt_dot(k, q)
    s_ref[...] = s
    return jnp.max(s, axis=0, keepdims=True)


def _softmax_step_t(s_ref, s_max, vt, m_prev, l_prev, acc_ref, idx):
    m_new = jnp.maximum(m_prev, s_max)
    alpha = jnp.exp2(m_prev - m_new)
    p = jnp.exp2(s_ref[...] - m_new)
    l_new = None if l_prev is None else alpha * l_prev + jnp.sum(p, axis=0, keepdims=True)
    acc_ref[idx] = acc_ref[idx] * alpha + jnp.dot(vt, p.astype(vt.dtype), preferred_element_type=F32)
    return m_new, l_new


def _pipelined_key_loop(nk, scores, consume, init):
    assert nk % 2 == 0

    def pair(jj, carry):
        state, smax0 = carry
        j = 2 * jj
        smax1 = scores(1, j + 1)
        state = consume(0, j, smax0, state)
        smax0 = scores(0, j + 2)
        state = consume(1, j + 1, smax1, state)
        return state, smax0

    state, smax0 = lax.fori_loop(0, nk // 2 - 1, pair, (init, scores(0, 0)))
    smax1 = scores(1, nk - 1)
    state = consume(0, nk - 2, smax0, state)
    return consume(1, nk - 1, smax1, state)


def _diff_attn_kernel(q_ref, k_ref, vt_ref, lq1_ref, lk1_ref, lq2_ref, lk2_ref, g_ref, o_ref, acc_ref, s_ref,
                      *, lam_init):
    tq = q_ref.shape[0]
    tk = vt_ref.shape[3]
    nk = k_ref.shape[0] // tk
    q = q_ref[...]
    acc_ref[...] = jnp.zeros_like(acc_ref)
    subs = [slice(sub * A_DH, (sub + 1) * A_DH) for sub in range(2)]

    def scores(slot, j):
        k = k_ref[pl.ds(pl.multiple_of(j * tk, tk), tk), :]
        return tuple(_scores_t(k[:, sl], q[:, sl], s_ref.at[slot, sub]) for sub, sl in enumerate(subs))

    def consume(slot, j, smax, state):
        vt = vt_ref[0, j]
        m0, l0, m1, l1 = state
        m0, l0 = _softmax_step_t(s_ref.at[slot, 0], smax[0], vt, m0, l0, acc_ref, 0)
        m1, l1 = _softmax_step_t(s_ref.at[slot, 1], smax[1], vt, m1, l1, acc_ref, 1)
        return m0, l0, m1, l1

    neg = jnp.full((1, tq), -jnp.inf, F32)
    zero = jnp.zeros((1, tq), F32)
    _, l0, _, l1 = _pipelined_key_loop(nk, scores, consume, (neg, zero, neg, zero))

    lam = (jnp.exp(jnp.sum(lq1_ref[...] * lk1_ref[...], axis=-1, keepdims=True))
           - jnp.exp(jnp.sum(lq2_ref[...] * lk2_ref[...], axis=-1, keepdims=True)) + lam_init)
    o = (acc_ref[0] * (1.0 / l0) - lam * (acc_ref[1] * (1.0 / l1))).T
    y = o * lax.rsqrt(jnp.mean(o * o, axis=-1, keepdims=True) + RMS_EPS) * g_ref[...]
    o_ref[...] = (y * (1.0 - lam_init)).astype(o_ref.dtype)


def _diff_attn(qk, vt, lq1, lk1, lq2, lk2, subln, nb, seq, lam_init, tq=1024):
    T = qk.shape[0]
    tk = vt.shape[3]
    tq = _tile(seq, tq)
    nq = seq // tq
    small = pl.BlockSpec((1, A_DH), lambda b, h, i: (0, 0))
    return pl.pallas_call(
        functools.partial(_diff_attn_kernel, lam_init=lam_init),
        grid=(nb, A_HEADS, nq),
        in_specs=[pl.BlockSpec((tq, A_HW), lambda b, h, i: (b * nq + i, h)),
                  pl.BlockSpec((seq, A_HW), lambda b, h, i: (b, A_HEADS + h)),
                  pl.BlockSpec((1, seq // tk, A_HW, tk), lambda b, h, i: (h, b, 0, 0)),
                  small, small, small, small,
                  pl.BlockSpec((1, A_HW), lambda b, h, i: (0, 0))],
        out_specs=pl.BlockSpec((tq, A_HW), lambda b, h, i: (b * nq + i, h)),
        out_shape=jax.ShapeDtypeStruct((T, A_W), BF16),
        scratch_shapes=[pltpu.VMEM((2, A_HW, tq), F32), pltpu.VMEM((2, 2, tk, tq), F32)],
        compiler_params=_params(("parallel", "parallel", "parallel")),
        name="diff_attn",
    )(qk, qk, vt, _row(lq1), _row(lk1), _row(lq2), _row(lk2), _row(subln))


def _mla_attn_kernel(q_ref, k_ref, vt_ref, o_ref, acc_ref, s_ref):
    tq = q_ref.shape[0]
    tk = vt_ref.shape[3]
    nk = k_ref.shape[0] // tk
    q = q_ref[...]
    acc_ref[...] = jnp.zeros_like(acc_ref)

    def scores(slot, j):
        return _scores_t(k_ref[pl.ds(pl.multiple_of(j * tk, tk), tk), :], q, s_ref.at[slot])

    def consume(slot, j, smax, m):
        return _softmax_step_t(s_ref.at[slot], smax, vt_ref[0, j], m, None, acc_ref, 0)[0]

    _pipelined_key_loop(nk, scores, consume, jnp.full((1, tq), -jnp.inf, F32))
    l = acc_ref[0, C_VDIM:C_VDIM + 1, :]
    o_ref[...] = (acc_ref[0, :C_VDIM, :] * (1.0 / l)).T.astype(o_ref.dtype)


def _mla_attn(qc, kc, vt, nb, seq, tq=2048):
    T = qc.shape[0]
    tk = vt.shape[3]
    tq = _tile(seq, tq)
    nq = seq // tq
    return pl.pallas_call(
        _mla_attn_kernel,
        grid=(nb, C_HEADS, nq),
        in_specs=[pl.BlockSpec((tq, C_HPAD), lambda b, h, i: (b * nq + i, h)),
                  pl.BlockSpec((seq, C_HPAD), lambda b, h, i: (b, h)),
                  pl.BlockSpec((1, seq // tk, C_VT_ROWS, tk), lambda b, h, i: (h, b, 0, 0))],
        out_specs=pl.BlockSpec((tq, C_VDIM), lambda b, h, i: (b * nq + i, h)),
        out_shape=jax.ShapeDtypeStruct((T, C_OUT), BF16),
        scratch_shapes=[pltpu.VMEM((1, C_VT_ROWS, tq), F32), pltpu.VMEM((2, tk, tq), F32)],
        compiler_params=_params(("parallel", "parallel", "parallel")),
        name="mla_attn",
    )(qc, kc, vt)


def _window_attn_kernel(q_ref, kp_ref, km_ref, kn_ref, vp_ref, vm_ref, vn_ref, o_ref, lse_ref, *, half, sub_len):
    tq = q_ref.shape[0]
    i = pl.program_id(1)
    span = tq + 2 * half
    qpos = i * tq + lax.broadcasted_iota(jnp.int32, (tq, span), 0)
    kpos = i * tq - half + lax.broadcasted_iota(jnp.int32, (tq, span), 1)
    valid = (jnp.abs(kpos - qpos) <= half) & (kpos >= 0) & (kpos < sub_len)
    scale = B_DH ** -0.5
    for h in range(B_HPG):
        sl = slice(h * B_DH, (h + 1) * B_DH)
        k = jnp.concatenate([kp_ref[:, sl], km_ref[:, sl], kn_ref[:, sl]], axis=0)
        v = jnp.concatenate([vp_ref[:, sl], vm_ref[:, sl], vn_ref[:, sl]], axis=0)
        s = jnp.where(valid, _nt_dot(q_ref[:, sl], k) * scale, NEG_BIG)
        m = jnp.max(s, axis=-1, keepdims=True)
        p = jnp.exp(s - m)
        l = jnp.sum(p, axis=-1, keepdims=True)
        o = jnp.dot(p.astype(v.dtype), v, preferred_element_type=F32) / l
        o_ref[:, sl] = o.astype(o_ref.dtype)
        lse_ref[:, sl] = jnp.broadcast_to(m + jnp.log(l), (tq, B_DH))


def _window_attn(qkv, g, tq=256):
    win, dil = B_GROUPS[g]
    half = win // (2 * dil)
    nsub, L, _ = qkv.shape
    tq = _tile(L, tq)
    assert tq % half == 0
    nq = L // tq
    r = tq // half
    nh = L // half
    flat = qkv.reshape(nsub * L, 3 * B_GW)

    def main(col):
        return pl.BlockSpec((tq, B_GW), lambda s, i: (s * nq + i, col))

    def prev(col):
        return pl.BlockSpec((half, B_GW), lambda s, i: (s * nh + jnp.maximum(i * r - 1, 0), col))

    def nxt(col):
        return pl.BlockSpec((half, B_GW), lambda s, i: (s * nh + jnp.minimum((i + 1) * r, nh - 1), col))

    out_spec = pl.BlockSpec((tq, B_GW), lambda s, i: (s * nq + i, 0))
    o, lse = pl.pallas_call(
        functools.partial(_window_attn_kernel, half=half, sub_len=L),
        grid=(nsub, nq),
        in_specs=[main(0), prev(1), main(1), nxt(1), prev(2), main(2), nxt(2)],
        out_specs=[out_spec, out_spec],
        out_shape=[jax.ShapeDtypeStruct((nsub * L, B_GW), BF16), jax.ShapeDtypeStruct((nsub * L, B_GW), F32)],
        compiler_params=_params(("parallel", "parallel")),
        name=f"window_attn_d{dil}",
    )(flat, flat, flat, flat, flat, flat, flat)
    return o.reshape(nsub, L, B_GW), lse.reshape(nsub, L, B_GW)


def _combine_kernel(o0, o1, o2, l0, l1, l2, o_ref, so, sl):
    tm = o_ref.shape[0]
    heads = [slice(h * B_DH, (h + 1) * B_DH) for h in range(B_HPG)]
    for g, (og, lg) in enumerate(((o0, l0), (o1, l1), (o2, l2))):
        dil = og.shape[0]
        n = tm // dil
        for c in range(dil):
            for h, hs in enumerate(heads):
                so[g * B_HPG + h, pl.ds(c, n, stride=dil), :] = og[c, :, hs].astype(F32)
                sl[g * B_HPG + h, pl.ds(c, n, stride=dil), :] = lg[c, :, hs]
    for h, hs in enumerate(heads):
        a0, a1, a2 = sl[h], sl[B_HPG + h], sl[2 * B_HPG + h]
        m = jnp.maximum(jnp.maximum(a0, a1), a2)
        e0, e1, e2 = jnp.exp(a0 - m), jnp.exp(a1 - m), jnp.exp(a2 - m)
        num = e0 * so[h] + e1 * so[B_HPG + h] + e2 * so[2 * B_HPG + h]
        o_ref[:, hs] = (num / (e0 + e1 + e2)).astype(o_ref.dtype)


def _combine(outs, lses, nb, seq, tm=512):
    tm = _tile(seq, tm)
    npos = seq // tm
    ng = len(outs)

    def spec(a):
        dil = a.shape[0] // nb
        return pl.BlockSpec((dil, tm // dil, B_GW), lambda i: (i // npos, i % npos, 0))

    return pl.pallas_call(
        _combine_kernel,
        grid=(nb * npos,),
        in_specs=[spec(a) for a in outs] + [spec(a) for a in lses],
        out_specs=pl.BlockSpec((tm, B_GW), lambda i: (i, 0)),
        out_shape=jax.ShapeDtypeStruct((nb * seq, B_GW), BF16),
        scratch_shapes=[pltpu.VMEM((ng * B_HPG, tm, B_DH), F32), pltpu.VMEM((ng * B_HPG, tm, B_DH), F32)],
        compiler_params=_params(("parallel",)),
        name="window_combine",
    )(*outs, *lses)


def _merge_kernel(u_ref, oa_ref, ob_ref, oc_ref, wga_ref, wgb_ref, wgc_ref, wa_ref, wb_ref, wc_ref, o_ref):
    u = u_ref[...]

    def branch(wg_ref, x_ref, w_ref):
        gate = jax.nn.sigmoid(jnp.dot(u, wg_ref[...], preferred_element_type=F32))
        return gate * jnp.dot(x_ref[...], w_ref[...], preferred_element_type=F32)

    o_ref[...] = (branch(wga_ref, oa_ref, wa_ref) + branch(wgb_ref, ob_ref, wb_ref)
                  + branch(wgc_ref, oc_ref, wc_ref)).astype(o_ref.dtype)


def _merge(u, oa, ob, oc, w_gates, wa, wb, wc, tm=768, tn=256):
    T, D = u.shape
    tm, tn = _tile(T, tm), _tile(D, tn)
    nj = D // tn
    row = lambda w: pl.BlockSpec((tm, w), lambda i, j: (i, 0))
    col = lambda k: pl.BlockSpec((k, tn), lambda i, j: (0, j))
    gate = lambda n: pl.BlockSpec((D, tn), lambda i, j: (0, n * nj + j))
    return pl.pallas_call(
        _merge_kernel,
        grid=(T // tm, nj),
        in_specs=[row(D), row(oa.shape[1]), row(ob.shape[1]), row(oc.shape[1]),
                  gate(0), gate(1), gate(2), col(wa.shape[0]), col(wb.shape[0]), col(wc.shape[0])],
        out_specs=pl.BlockSpec((tm, tn), lambda i, j: (i, j)),
        out_shape=jax.ShapeDtypeStruct((T, D), BF16),
        compiler_params=_params(("parallel", "parallel")),
        name="gated_merge",
    )(u, oa, ob, oc, w_gates, w_gates, w_gates, wa, wb, wc)


def _rope_tables(seq):
    pos = jnp.arange(seq, dtype=F32)[:, None]

    def ang(d):
        half = d // 2
        inv = 1.0 / (ROPE_THETA ** (jnp.arange(half, dtype=F32) * (2.0 / d)))
        return pos * inv[None, :]

    a = ang(A_DH)
    cos128 = jnp.concatenate([jnp.cos(a), jnp.cos(a)], axis=-1)
    sin128 = jnp.concatenate([-jnp.sin(a), jnp.sin(a)], axis=-1)
    c = ang(C_ROPE)
    z = jnp.zeros_like(c)
    cos64 = jnp.concatenate([jnp.cos(c), jnp.cos(c), z, z], axis=-1)
    sin_lo = jnp.concatenate([-jnp.sin(c), z, z, z], axis=-1)
    sin_hi = jnp.concatenate([z, jnp.sin(c), z, z], axis=-1)
    return cos128, sin128, cos64, sin_lo, sin_hi


def _layer(x, l, nb, seq, tabs, p, out_rows):
    cos128, sin128, cos64, sin_lo, sin_hi = tabs
    D = x[0].shape[1]
    ff = p["ffn1_wg"].shape[2]
    ffp = -(-ff // 1024) * 1024

    def ffn(x, norm, wg, wu, wd, out_rows=None):
        h = _rmsnorm(x, norm[l])
        act = _ffn_up(h, _cast(wg, l, cols_out=ffp), _cast(wu, l, cols_out=ffp))
        return _mm_res(act, _cast(wd, l, rows_out=ffp), x, 0.5, out_rows, tk=ffp // 4)

    x = ffn(x, p["ffn1_norm"], p["ffn1_wg"], p["ffn1_wu"], p["ffn1_wd"])

    u = _rmsnorm(x, p["mix_norm"][l])
    w_in_t = jnp.swapaxes(p["w_in"], 1, 2)
    b0 = 3 * A_W
    tile_h = lambda g, n: jnp.tile(g, n)

    a_gains = jnp.concatenate([tile_h(p["a_qnorm"][l] * (A_DH ** -0.5 * LOG2E), A_W // A_DH),
                               tile_h(p["a_knorm"][l], A_W // A_DH)]).reshape(1, 2 * A_W)
    a_qk = _proj_rope(u, _cast_t(w_in_t, l, 0, 2 * A_W), a_gains, cos128, sin128, seq)
    a_vt = _proj_vt(u, _cast_t(w_in_t, l, 2 * A_W, A_W), A_HEADS, ATT_TK)
    lam_init = 0.8 - 0.6 * math.exp(-0.3 * l)
    oa = _diff_attn(a_qk, a_vt, p["a_lq1"][l], p["a_lk1"][l], p["a_lq2"][l], p["a_lk2"][l], p["a_subln"][l],
                    nb, seq, lam_init)

    b_gains = jnp.concatenate([tile_h(p["b_qnorm"][l], B_HPG), tile_h(p["b_knorm"][l], B_HPG),
                               jnp.ones((B_GW,), F32)]).reshape(1, 3 * B_GW)
    w_b = _cast_t(w_in_t, l, b0, 3 * B_W, tc=256)
    wres = [_window_attn(_proj_b(u, w_b, b_gains, cos128, sin128, g, nb, seq), g) for g in range(len(B_GROUPS))]
    ob = _combine([r[0] for r in wres], [r[1] for r in wres], nb, seq)

    pc = _mm(u, _cast_t(w_in_t, l, AB_COLS, C_COLS, cols_out=C_COLS_PAD, tc=256), F32)
    wq = p["c_wq_up"][l].reshape(C_Q_RANK, C_HEADS, C_DQK)
    wq_pad = jnp.pad(wq, ((0, 0), (0, 0), (0, C_HPAD - C_DQK))).reshape(C_Q_RANK, C_HEADS * C_HPAD).astype(BF16)
    qg = p["c_qnorm"][l] * (C_DQK ** -0.5 * LOG2E)
    kg = p["c_knorm"][l]
    pad_to = lambda v, n: jnp.pad(v, (0, n - v.shape[0]))
    qc = _mla_q(pc, _row(p["c_qa_norm"][l]), wq_pad, _row(pad_to(qg, C_HPAD)), cos64, sin_lo, sin_hi, seq)
    kc, c_vt = _mla_kv(pc, _row(p["c_kva_norm"][l]), p["c_wkv_up"][l].astype(BF16), _row(kg[:C_NOPE]),
                       _row(pad_to(kg[C_NOPE:], LANES)), cos64, sin_lo, sin_hi, seq, ATT_TK)
    oc = _mla_attn(qc, kc, c_vt, nb, seq)

    m = _merge(u, oa, ob, oc, _cast_t(w_in_t, l, AB_COLS + C_COLS, N_BRANCH * D), _cast(p["w_br_a"], l),
               _cast(p["w_br_b"], l), _cast(p["w_br_c"], l))
    x = _mm_res(m, _cast(p["w_out"], l), x, 1.0, tm=1024, tn=512, tk=D)

    return ffn(x, p["ffn2_norm"], p["ffn2_wg"], p["ffn2_wu"], p["ffn2_wd"], out_rows)


def kernel(x_prompt, x_sample, ffn1_norm, ffn1_wg, ffn1_wu, ffn1_wd, mix_norm, w_in, a_qnorm, a_knorm, a_lq1, a_lk1, a_lq2, a_lk2, a_subln, b_qnorm, b_knorm, c_qa_norm, c_kva_norm, c_wq_up, c_wkv_up, c_qnorm, c_knorm, w_br_a, w_br_b, w_br_c, w_out, ffn2_norm, ffn2_wg, ffn2_wu, ffn2_wd):
    p = dict(ffn1_norm=ffn1_norm, ffn1_wg=ffn1_wg, ffn1_wu=ffn1_wu, ffn1_wd=ffn1_wd, mix_norm=mix_norm, w_in=w_in,
             a_qnorm=a_qnorm, a_knorm=a_knorm, a_lq1=a_lq1, a_lk1=a_lk1, a_lq2=a_lq2, a_lk2=a_lk2, a_subln=a_subln,
             b_qnorm=b_qnorm, b_knorm=b_knorm, c_qa_norm=c_qa_norm, c_kva_norm=c_kva_norm, c_wq_up=c_wq_up,
             c_wkv_up=c_wkv_up, c_qnorm=c_qnorm, c_knorm=c_knorm, w_br_a=w_br_a, w_br_b=w_br_b, w_br_c=w_br_c,
             w_out=w_out, ffn2_norm=ffn2_norm, ffn2_wg=ffn2_wg, ffn2_wu=ffn2_wu, ffn2_wd=ffn2_wd)
    bp, seq, D = x_prompt.shape
    bs, seq_s, _ = x_sample.shape
    assert seq == seq_s
    nb = bp + bs
    x = (x_prompt.reshape(bp * seq, D), x_sample.reshape(bs * seq, D))
    tabs = _rope_tables(seq)
    depth = ffn1_norm.shape[0]
    for l in range(depth):
        x = _layer(x, l, nb, seq, tabs, p, (bp * seq, bs * seq) if l == depth - 1 else None)
    return x[0].reshape(bp, seq, D), x[1].reshape(bs, seq, D)
```
